```python
import math
import jax, jax.numpy as jnp
from jax import lax
import numpy as np

D_MODEL = 1024
BATCH = 32
SEQ = 2048
DEPTH = 1

D_MIX = D_MODEL
DIFF_HEADS = 4
DIFF_QK_DIM = 64
DIFF_V_DIM = 2 * DIFF_QK_DIM
DIFF_WIDTH = DIFF_HEADS * DIFF_V_DIM
DSA_HEADS = 4
DSA_LATENT = 128
DSA_V_DIM = 128
DSA_WIDTH = DSA_HEADS * DSA_V_DIM
IDX_HEADS = 8
IDX_DIM = 64
TOPK_MAX = 256
REL_BUCKETS = 32
REL_MAX_DIST = 128
N_ATTN_HEADS = DIFF_HEADS + DSA_HEADS
PEER_HEADS = 8
PEER_NKEYS = 128
PEER_QDIM = 128
PEER_HALF = PEER_QDIM // 2
PEER_TOPK = 16
N_EXPERTS = PEER_NKEYS * PEER_NKEYS
PEER_CHUNK = 128
Q_BLOCK = 128
RMS_EPS = 1e-6
D_IN_PROJ = (2 * DIFF_HEADS * 2 * DIFF_QK_DIM + DIFF_WIDTH + DSA_HEADS * DSA_LATENT
             + DSA_LATENT + IDX_HEADS * IDX_DIM + IDX_DIM + IDX_HEADS)

kernel_name = "hymba_diffattn_dsa_peer_block"


def rms_norm(x, g):
    xf = x.astype(jnp.float32)
    y = xf * lax.rsqrt(jnp.mean(xf * xf, axis=-1, keepdims=True) + RMS_EPS)
    return (y * g.astype(jnp.float32)).astype(x.dtype)


def rel_bucket(dist):
    n = jnp.maximum(dist, 0)
    max_exact = REL_BUCKETS // 2
    nf = jnp.maximum(n, max_exact).astype(jnp.float32)
    large = max_exact + (jnp.log(nf / max_exact) / math.log(REL_MAX_DIST / max_exact)
                         * (REL_BUCKETS - max_exact)).astype(jnp.int32)
    large = jnp.minimum(large, REL_BUCKETS - 1)
    return jnp.where(n < max_exact, n, large)


def to_blocks(a):
    B, S = a.shape[0], a.shape[1]
    a = a.reshape((B, S // Q_BLOCK, Q_BLOCK) + a.shape[2:])
    return jnp.moveaxis(a, 1, 0)


def from_blocks(a):
    a = jnp.moveaxis(a, 0, 1)
    return a.reshape((a.shape[0], a.shape[1] * a.shape[2]) + a.shape[3:])


def diff_attention(q1, q2, k1, k2, v, lam, bias_table):
    S = q1.shape[1]
    nb = S // Q_BLOCK
    scale = DIFF_QK_DIM ** -0.5
    s_pos = jnp.arange(S, dtype=jnp.int32)

    def block(args):
        q1b, q2b, blk = args
        t_pos = blk * Q_BLOCK + jnp.arange(Q_BLOCK, dtype=jnp.int32)
        dist = t_pos[:, None] - s_pos[None, :]
        bias = jnp.transpose(bias_table[rel_bucket(dist)], (2, 0, 1)).astype(jnp.float32)
        mask = dist >= 0
        l1 = jnp.einsum('bqhd,bshd->bhqs', q1b, k1).astype(jnp.float32) * scale + bias
        l2 = jnp.einsum('bqhd,bshd->bhqs', q2b, k2).astype(jnp.float32) * scale + bias
        p1 = jax.nn.softmax(jnp.where(mask, l1, -jnp.inf), axis=-1)
        p2 = jax.nn.softmax(jnp.where(mask, l2, -jnp.inf), axis=-1)
        attn = (p1 - lam * p2).astype(v.dtype)
        return jnp.einsum('bhqs,bshd->bqhd', attn, v)

    out = lax.map(block, (to_blocks(q1), to_blocks(q2), jnp.arange(nb, dtype=jnp.int32)))
    return from_blocks(out)


def dsa_attention(q, kv, iq, ik, iw, w_uv, bias_table):
    S = q.shape[1]
    nb = S // Q_BLOCK
    k_top = min(TOPK_MAX, S // 4)
    scale = DSA_LATENT ** -0.5
    idx_scale = IDX_DIM ** -0.5
    s_pos = jnp.arange(S, dtype=jnp.int32)
    gather_rows = jax.vmap(lambda a, i: a[i])

    def block(args):
        qb, iqb, iwb, blk = args
        t_pos = blk * Q_BLOCK + jnp.arange(Q_BLOCK, dtype=jnp.int32)
        dots = jax.nn.relu(jnp.einsum('bqhd,bsd->bqhs', iqb, ik).astype(jnp.float32) * idx_scale)
        score = jnp.einsum('bqhs,bqh->bqs', dots, iwb.astype(jnp.float32))
        score = jnp.where(s_pos[None, None, :] <= t_pos[None, :, None], score, -jnp.inf)
        _, sel = lax.top_k(score, k_top)
        dist = t_pos[None, :, None] - sel
        valid = dist >= 0
        kv_sel = gather_rows(kv, sel)
        bias = jnp.transpose(bias_table[rel_bucket(dist)], (0, 3, 1, 2)).astype(jnp.float32)
        logits = jnp.einsum('bqhd,bqkd->bhqk', qb, kv_sel).astype(jnp.float32) * scale + bias
        p = jax.nn.softmax(jnp.where(valid[:, None], logits, -jnp.inf), axis=-1).astype(kv.dtype)
        o = jnp.einsum('bhqk,bqkd->bqhd', p, kv_sel)
        return jnp.einsum('bqhc,hcd->bqhd', o, w_uv)

    out = lax.map(block, (to_blocks(q), to_blocks(iq), to_blocks(iw), jnp.arange(nb, dtype=jnp.int32)))
    return from_blocks(out)


def peer(h, w_q, keys, u_tab, v_tab):
    B, S, D = h.shape
    T = B * S
    xc = h.reshape(T // PEER_CHUNK, PEER_CHUNK, D)

    def chunk(xt):
        q = (xt @ w_q).reshape(PEER_CHUNK, PEER_HEADS, 2, PEER_HALF)
        sc = jnp.einsum('thpd,hpnd->thpn', q, keys).astype(jnp.float32)
        s1, i1 = lax.top_k(sc[:, :, 0], PEER_TOPK)
        s2, i2 = lax.top_k(sc[:, :, 1], PEER_TOPK)
        cand = (s1[..., :, None] + s2[..., None, :]).reshape(PEER_CHUNK, PEER_HEADS, PEER_TOPK * PEER_TOPK)
        cand_idx = (i1[..., :, None] * PEER_NKEYS + i2[..., None, :]).reshape(PEER_CHUNK, PEER_HEADS, PEER_TOPK * PEER_TOPK)
        top, pos = lax.top_k(cand, PEER_TOPK)
        e_idx = jnp.take_along_axis(cand_idx, pos, axis=-1)
        g = jax.nn.softmax(top, axis=-1)
        u_sel = u_tab[e_idx]
        v_sel = v_tab[e_idx]
        act = jax.nn.gelu(jnp.einsum('td,thkd->thk', xt, u_sel).astype(jnp.float32), approximate=False)
        return jnp.einsum('thk,thkd->td', (g * act).astype(xt.dtype), v_sel)

    return lax.map(chunk, xc).reshape(B, S, D)


def setup_inputs(seed: int = 0) -> dict:
    key = jax.random.key(seed)
    ks = jax.random.split(key, 24)
    f = jnp.float32
    D = D_MODEL
    nrm = lambda k, shape, s: jax.random.normal(k, shape, f) * s
    return {
        "x": nrm(ks[0], (BATCH, SEQ, D), 1.0),
        "c": nrm(ks[1], (BATCH, D), 1.0),
        "w_ada": nrm(ks[2], (DEPTH, D, 6 * D), 0.5 * D ** -0.5),
        "b_ada": nrm(ks[3], (DEPTH, 6 * D), 0.01),
        "g_norm_mix": 1.0 + nrm(ks[4], (DEPTH, D), 0.02),
        "w_in": nrm(ks[5], (DEPTH, D, D_IN_PROJ), D ** -0.5),
        "lam_q1": nrm(ks[6], (DEPTH, DIFF_QK_DIM), 0.1),
        "lam_k1": nrm(ks[7], (DEPTH, DIFF_QK_DIM), 0.1),
        "lam_q2": nrm(ks[8], (DEPTH, DIFF_QK_DIM), 0.1),
        "lam_k2": nrm(ks[9], (DEPTH, DIFF_QK_DIM), 0.1),
        "g_subln": 1.0 + nrm(ks[10], (DEPTH, DIFF_V_DIM), 0.02),
        "g_kv_norm": 1.0 + nrm(ks[11], (DEPTH, DSA_LATENT), 0.02),
        "w_uv": nrm(ks[12], (DEPTH, DSA_HEADS, DSA_LATENT, DSA_V_DIM), DSA_LATENT ** -0.5),
        "w_out": nrm(ks[13], (DEPTH, D_MIX, D), D_MIX ** -0.5),
        "g_norm_ffn": 1.0 + nrm(ks[14], (DEPTH, D), 0.02),
        "w_peer_q": nrm(ks[15], (DEPTH, D, PEER_HEADS * PEER_QDIM), D ** -0.5),
        "peer_keys": nrm(ks[16], (DEPTH, PEER_HEADS, 2, PEER_NKEYS, PEER_HALF), PEER_HALF ** -0.5),
        "peer_u": nrm(ks[17], (DEPTH, N_EXPERTS, D), D ** -0.5),
        "peer_v": nrm(ks[18], (DEPTH, N_EXPERTS, D), 1.0),
        "rel_bias": nrm(ks[19], (REL_BUCKETS, N_ATTN_HEADS), 0.2),
        "g_final": 1.0 + nrm(ks[20], (D,), 0.02),
    }


def reference(x, c, w_ada, b_ada, g_norm_mix, w_in, lam_q1, lam_k1, lam_q2, lam_k2,
              g_subln, g_kv_norm, w_uv, w_out, g_norm_ffn, w_peer_q, peer_keys,
              peer_u, peer_v, rel_bias, g_final):
    B, S, D = x.shape
    sizes = (DIFF_HEADS * 2 * DIFF_QK_DIM, DIFF_HEADS * 2 * DIFF_QK_DIM, DIFF_WIDTH,
             DSA_HEADS * DSA_LATENT, DSA_LATENT, IDX_HEADS * IDX_DIM, IDX_DIM, IDX_HEADS)
    offsets = [int(o) for o in np.cumsum(sizes)[:-1]]
    bias_diff = rel_bias[:, :DIFF_HEADS]
    bias_dsa = rel_bias[:, DIFF_HEADS:]
    c_act = jax.nn.silu(c)

    for l in range(DEPTH):
        mod = c_act @ w_ada[l] + b_ada[l]
        shift_a, scale_a, gate_a, shift_f, scale_f, gate_f = [m[:, None, :] for m in jnp.split(mod, 6, axis=-1)]

        h = rms_norm(x, g_norm_mix[l]) * (1.0 + scale_a) + shift_a
        proj = h @ w_in[l]
        dq, dk, dv, sq, skv, iq, ik, iw = jnp.split(proj, offsets, axis=-1)

        dq = dq.reshape(B, S, DIFF_HEADS, 2, DIFF_QK_DIM)
        dk = dk.reshape(B, S, DIFF_HEADS, 2, DIFF_QK_DIM)
        dv = dv.reshape(B, S, DIFF_HEADS, DIFF_V_DIM)
        lambda_init = 0.8 - 0.6 * math.exp(-0.3 * l)
        lam = (jnp.exp(jnp.sum(lam_q1[l].astype(jnp.float32) * lam_k1[l].astype(jnp.float32)))
               - jnp.exp(jnp.sum(lam_q2[l].astype(jnp.float32) * lam_k2[l].astype(jnp.float32)))
               + lambda_init)
        o_diff = diff_attention(dq[:, :, :, 0], dq[:, :, :, 1], dk[:, :, :, 0], dk[:, :, :, 1], dv, lam, bias_diff)
        o_diff = (rms_norm(o_diff, g_subln[l]) * (1.0 - lambda_init)).reshape(B, S, DIFF_WIDTH)

        sq = sq.reshape(B, S, DSA_HEADS, DSA_LATENT)
        skv = rms_norm(skv, g_kv_norm[l])
        iq = iq.reshape(B, S, IDX_HEADS, IDX_DIM)
        iw = iw * (IDX_HEADS ** -0.5)
        o_dsa = dsa_attention(sq, skv, iq, ik, iw, w_uv[l], bias_dsa).reshape(B, S, DSA_WIDTH)

        mix = jnp.concatenate([o_diff, o_dsa], axis=-1) @ w_out[l]
        x = x + gate_a * mix

        h2 = rms_norm(x, g_norm_ffn[l]) * (1.0 + scale_f) + shift_f
        x = x + gate_f * peer(h2, w_peer_q[l], peer_keys[l], peer_u[l], peer_v[l])

    return rms_norm(x, g_final)
```

```python
import functools
import math

import jax
import jax.numpy as jnp
import numpy as np
from jax import lax
from jax.experimental import pallas as pl
from jax.experimental.pallas import tpu as pltpu

F32 = jnp.float32
BF16 = jnp.bfloat16
I32 = jnp.int32

DIFF_HEADS = 4
DIFF_QK = 64
DIFF_V = 128
DSA_HEADS = 4
DSA_LATENT = 128
DSA_V = 128
IDX_HEADS = 8
IDX_DIM = 64
TOPK_MAX = 256
REL_BUCKETS = 32
REL_MAX_DIST = 128
PEER_HEADS = 8
PEER_NKEYS = 128
PEER_HALF = 64
PEER_TOPK = 16
RMS_EPS = 1e-6

NEG = -1e30
INT_MIN = -(2 ** 31)
VMEM_LIMIT = 56 * 1024 * 1024

ATT_TILE = 256
PEER_ROUTE_TILE = 256
PEER_TOK_TILE = 64
PEER_ROWS = PEER_HEADS * PEER_TOPK


def _cparams(sem):
    return pltpu.CompilerParams(dimension_semantics=sem, vmem_limit_bytes=VMEM_LIMIT)


def _dot(a, b):
    return jnp.dot(a, b, preferred_element_type=F32)


def _dot_nt(a, b):
    return lax.dot_general(a, b, (((1,), (1,)), ((), ())), preferred_element_type=F32)


def _dot_tn(a, b):
    return lax.dot_general(a, b, (((0,), (0,)), ((), ())), preferred_element_type=F32)


def _adaln_kernel(c_ref, w_ref, b_ref, o_ref):
    c = c_ref[...]
    ca = c * (1.0 / (1.0 + jnp.exp(-c)))
    o_ref[...] = _dot(ca, w_ref[...]) + b_ref[...]


def _adaln(c, w, b):
    B, D = c.shape
    N = w.shape[1]
    tn = 1024
    return pl.pallas_call(
        _adaln_kernel,
        grid=(N // tn,),
        in_specs=[pl.BlockSpec((B, D), lambda j: (0, 0)),
                  pl.BlockSpec((D, tn), lambda j: (0, j)),
                  pl.BlockSpec((1, tn), lambda j: (0, j))],
        out_specs=pl.BlockSpec((B, tn), lambda j: (0, j)),
        out_shape=jax.ShapeDtypeStruct((B, N), F32),
        compiler_params=_cparams(("arbitrary",)),
        name="adaln",
    )(c, w, b.reshape(1, N))


def _inproj_kernel(x_ref, sc_ref, sh_ref, g_ref,
                   wn_dk, wn_kv, wn_ik, wt_dq, wt_dv, wt_sq, wt_kv, wt_iq, wt_iw,
                   gkv_row, gkv_col,
                   dk_o, kvn_o, ik_o, dqT_o, dvT_o, sqT_o, kvT_o, iqT_o, iwT_o):
    x = x_ref[0]
    ms = jnp.mean(x * x, axis=-1, keepdims=True)
    h = (x * lax.rsqrt(ms + RMS_EPS)) * g_ref[...]
    h = h * (1.0 + sc_ref[0]) + sh_ref[0]
    hb = h.astype(BF16)

    dk_o[0] = _dot(hb, wn_dk[...]).astype(BF16)
    kv = _dot(hb, wn_kv[...])
    kv = kv * lax.rsqrt(jnp.mean(kv * kv, axis=-1, keepdims=True) + RMS_EPS) * gkv_row[...]
    kvn_o[0] = kv.astype(BF16)
    ik_o[0] = _dot(hb, wn_ik[...]).astype(BF16)

    dqT_o[0, 0] = (_dot_nt(wt_dq[...], hb) * (DIFF_QK ** -0.5)).astype(BF16)
    dvT_o[0, 0] = _dot_nt(wt_dv[...], hb).astype(BF16)
    sqT_o[0, 0] = (_dot_nt(wt_sq[...], hb) * (DSA_LATENT ** -0.5)).astype(BF16)
    kvT = _dot_nt(wt_kv[...], hb)
    kvT = kvT * lax.rsqrt(jnp.mean(kvT * kvT, axis=0, keepdims=True) + RMS_EPS) * gkv_col[...]
    kvT_o[0, 0] = kvT.astype(BF16)
    iqT_o[0, 0] = (_dot_nt(wt_iq[...], hb) * (IDX_DIM ** -0.5)).astype(BF16)
    iwT_o[0, 0] = _dot_nt(wt_iw[...], hb) * (IDX_HEADS ** -0.5)


def _inproj(x, mod3, g_norm, w_in, g_kv):
    B, S, D = x.shape
    tm = ATT_TILE
    nt = S // tm
    sizes = (512, 512, 512, 512, 128, 512, 64, 8)
    offs = np.cumsum((0,) + sizes)
    wb = w_in.astype(BF16)
    piece = lambda i: wb[:, offs[i]:offs[i + 1]]
    w_dq, w_dk, w_dv, w_sq, w_kv, w_iq, w_ik, w_iw = [piece(i) for i in range(8)]
    w_iwT = jnp.zeros((16, D), BF16).at[:8].set(w_iw.T)

    full = lambda a: pl.BlockSpec(a.shape, lambda b, i: (0,) * a.ndim)
    nat = lambda f: pl.BlockSpec((1, tm, f), lambda b, i: (b, i, 0))
    tr = lambda f: pl.BlockSpec((1, 1, f, tm), lambda b, i: (b, i, 0, 0))
    ins = [x, mod3, mod3, g_norm.reshape(1, D),
           w_dk, w_kv, w_ik, w_dq.T, w_dv.T, w_sq.T, w_kv.T, w_iq.T, w_iwT,
           g_kv.reshape(1, -1), g_kv.reshape(-1, 1)]
    in_specs = [pl.BlockSpec((1, tm, D), lambda b, i: (b, i, 0)),
                pl.BlockSpec((1, 1, D), lambda b, i: (b, 0, 1)),
                pl.BlockSpec((1, 1, D), lambda b, i: (b, 0, 0)),
                ] + [full(a) for a in ins[3:]]
    out_shape = [jax.ShapeDtypeStruct((B, S, 512), BF16),
                 jax.ShapeDtypeStruct((B, S, 128), BF16),
                 jax.ShapeDtypeStruct((B, S, 64), BF16),
                 jax.ShapeDtypeStruct((B, nt, 512, tm), BF16),
                 jax.ShapeDtypeStruct((B, nt, 512, tm), BF16),
                 jax.ShapeDtypeStruct((B, nt, 512, tm), BF16),
                 jax.ShapeDtypeStruct((B, nt, 128, tm), BF16),
                 jax.ShapeDtypeStruct((B, nt, 512, tm), BF16),
                 jax.ShapeDtypeStruct((B, nt, 16, tm), F32)]
    out_specs = [nat(512), nat(128), nat(64), tr(512), tr(512), tr(512), tr(128), tr(512), tr(16)]
    return pl.pallas_call(
        _inproj_kernel, grid=(B, nt), in_specs=in_specs, out_specs=out_specs, out_shape=out_shape,
        compiler_params=_cparams(("parallel", "parallel")), name="inproj",
    )(*ins)


def _softmax_step(logits, vT, m_ref, l_ref, acc_ref):
    m_prev = m_ref[...]
    m_new = jnp.maximum(m_prev, jnp.max(logits, axis=0, keepdims=True))
    alpha = jnp.exp(m_prev - m_new)
    p = jnp.exp(logits - m_new)
    l_ref[...] = alpha * l_ref[...] + jnp.sum(p, axis=0, keepdims=True)
    acc_ref[...] = alpha * acc_ref[...] + _dot(vT, p.astype(BF16))
    m_ref[...] = m_new


def _rel_bias_tiles(rel_bias_h, t):
    def bucket(dist):
        n = jnp.maximum(dist, 0)
        max_exact = REL_BUCKETS // 2
        nf = jnp.maximum(n, max_exact).astype(F32)
        large = max_exact + (jnp.log(nf / max_exact) / math.log(REL_MAX_DIST / max_exact)
                             * (REL_BUCKETS - max_exact)).astype(I32)
        large = jnp.minimum(large, REL_BUCKETS - 1)
        return jnp.where(n < max_exact, n, large)
    s = jnp.arange(t, dtype=I32)[:, None]
    q = jnp.arange(t, dtype=I32)[None, :]
    d_diag = q - s
    d_sub = q - s + t
    tb = rel_bias_h.astype(F32).T
    diag = jnp.where(d_diag >= 0, tb[:, bucket(d_diag)], NEG)
    sub = tb[:, bucket(d_sub)]
    far = tb[:, bucket(jnp.int32(2 * t))]
    return diag.astype(F32), sub.astype(F32), far.astype(F32)


def _diff_kernel(lam_ref, bfar_ref, kn_ref, qT_ref, vT_ref, bd_ref, bs_ref, gsub_ref, o_ref,
                 m1, l1, a1, m2, l2, a2, *, lambda_init):
    h = pl.program_id(1)
    qi = pl.program_id(2)
    tq = qT_ref.shape[-1]
    tk = tq
    qT = qT_ref[0, 0]
    row = lax.broadcasted_iota(I32, qT.shape, 0)
    zero = jnp.zeros_like(qT)
    q1 = jnp.where(row < DIFF_QK, qT, zero)
    q2 = jnp.where(row >= DIFF_QK, qT, zero)
    for m_, l_, a_ in ((m1, l1, a1), (m2, l2, a2)):
        m_[...] = jnp.full(m_.shape, NEG, F32)
        l_[...] = jnp.zeros(l_.shape, F32)
        a_[...] = jnp.zeros(a_.shape, F32)

    def step(ki, bias):
        kn = kn_ref[0, pl.ds(pl.multiple_of(ki * tk, tk), tk), :]
        vT = vT_ref[0, ki]
        _softmax_step(_dot(kn, q1) + bias, vT, m1, l1, a1)
        _softmax_step(_dot(kn, q2) + bias, vT, m2, l2, a2)

    bfar = bfar_ref[h]

    def far_body(ki, c):
        step(ki, bfar)
        return c
    lax.fori_loop(0, jnp.maximum(qi - 1, 0), far_body, 0)

    @pl.when(qi >= 1)
    def _():
        step(qi - 1, bs_ref[0])
    step(qi, bd_ref[0])

    lam = lam_ref[0]
    o = a1[...] / l1[...] - lam * (a2[...] / l2[...])
    o = o * lax.rsqrt(jnp.mean(o * o, axis=0, keepdims=True) + RMS_EPS) * gsub_ref[...]
    o_ref[0, 0] = (o * (1.0 - lambda_init)).astype(BF16)


def _diff_attention(dk, dqT, dvT, lam, bias_diff, g_subln, lambda_init):
    B, S, _ = dk.shape
    nt, t = dqT.shape[1], dqT.shape[3]
    H = DIFF_HEADS
    bd, bs, bfar = _rel_bias_tiles(bias_diff, t)
    smem = pl.BlockSpec(memory_space=pltpu.SMEM)
    return pl.pallas_call(
        functools.partial(_diff_kernel, lambda_init=lambda_init),
        grid=(B, H, nt),
        in_specs=[smem, smem,
                  pl.BlockSpec((1, S, 128), lambda b, h, i: (b, 0, h)),
                  pl.BlockSpec((1, 1, 128, t), lambda b, h, i: (b, i, h, 0)),
                  pl.BlockSpec((1, nt, 128, t), lambda b, h, i: (b, 0, h, 0)),
                  pl.BlockSpec((1, t, t), lambda b, h, i: (h, 0, 0)),
                  pl.BlockSpec((1, t, t), lambda b, h, i: (h, 0, 0)),
                  pl.BlockSpec((DIFF_V, 1), lambda b, h, i: (0, 0))],
        out_specs=pl.BlockSpec((1, 1, 128, t), lambda b, h, i: (b, i, h, 0)),
        out_shape=jax.ShapeDtypeStruct((B, nt, H * DIFF_V, t), BF16),
        scratch_shapes=[pltpu.VMEM((1, t), F32), pltpu.VMEM((1, t), F32), pltpu.VMEM((DIFF_V, t), F32),
                        pltpu.VMEM((1, t), F32), pltpu.VMEM((1, t), F32), pltpu.VMEM((DIFF_V, t), F32)],
        compiler_params=_cparams(("parallel", "parallel", "arbitrary")),
        name="diff_attention",
    )(lam.reshape(1), bfar, dk, dqT, dvT, bd, bs, g_subln.reshape(-1, 1).astype(F32))


def _dsa_kernel(bfar_ref, ikn_ref, iqT_ref, iwT_ref, kvn_ref, kvT_ref, sqT_ref, bd_ref, bs_ref,
                wuvT_ref, o_ref, key_s, madd_s, m_s, l_s, acc_s, *, k_top):
    qi = pl.program_id(1)
    t = iqT_ref.shape[-1]
    nblk = qi + 1

    iqT = iqT_ref[0, 0]
    iwT = iwT_ref[0, 0]
    srow = lax.broadcasted_iota(I32, (t, t), 0)
    tcol = lax.broadcasted_iota(I32, (t, t), 1)

    def score_body(ki, c):
        ik = ikn_ref[0, pl.ds(pl.multiple_of(ki * t, t), t), :]
        sc = jnp.zeros((t, t), F32)
        for hh in range(IDX_HEADS):
            d = _dot(ik, iqT[hh * IDX_DIM:(hh + 1) * IDX_DIM, :])
            sc = sc + jnp.maximum(d, 0.0) * iwT[hh:hh + 1, :]
        sc = jnp.where((ki < qi) | (srow <= tcol), sc, -jnp.inf)
        bits = lax.bitcast_convert_type(sc, I32)
        key_s[ki] = bits ^ ((bits >> 31) & jnp.int32(0x7FFFFFFF))
        return c
    lax.fori_loop(0, nblk, score_body, 0)

    def count_ge(cand):
        def body(ki, acc):
            ge = (key_s[ki] >= cand).astype(I32)
            return acc + jnp.sum(ge.reshape(t // 8, 8, t), axis=0)
        part = lax.fori_loop(0, nblk, body, jnp.zeros((8, t), I32))
        return jnp.sum(part, axis=0, keepdims=True)

    zero_t = jnp.zeros((1, t), I32)
    thr = jnp.where(count_ge(zero_t) >= k_top, zero_t, jnp.full((1, t), INT_MIN, I32))

    def bit_body(i, thr):
        cand = thr + (jnp.int32(1) << (30 - i))
        return jnp.where(count_ge(cand) >= k_top, cand, thr)
    thr = lax.fori_loop(0, 31, bit_body, thr)

    def count_gt_eq(ki, acc):
        k = key_s[ki]
        gt = (k > thr).astype(I32)
        eq = (k == thr).astype(I32)
        return (acc[0] + jnp.sum(gt.reshape(t // 8, 8, t), axis=0),
                acc[1] + jnp.sum(eq.reshape(t // 8, 8, t), axis=0))
    z8 = jnp.zeros((8, t), I32)
    gt8, eq8 = lax.fori_loop(0, nblk, count_gt_eq, (z8, z8))
    need = k_top - jnp.sum(gt8, axis=0, keepdims=True)
    n_eq = jnp.sum(eq8, axis=0, keepdims=True)
    has_ties = jnp.max(jnp.where(n_eq > need, 1, 0)) > 0

    def count_eq_le(j):
        def body(ki, acc):
            pos = srow + ki * t
            hit = ((key_s[ki] == thr) & (pos <= j)).astype(I32)
            return acc + jnp.sum(hit.reshape(t // 8, 8, t), axis=0)
        part = lax.fori_loop(0, nblk, body, jnp.zeros((8, t), I32))
        return jnp.sum(part, axis=0, keepdims=True)

    def tie_search():
        nbits = max(1, int(math.ceil(math.log2(key_s.shape[0] * t))))

        def body(i, lo):
            cand = lo + (jnp.int32(1) << (nbits - 1 - i))
            return jnp.where(count_eq_le(cand - 1) < need, cand, lo)
        return lax.fori_loop(0, nbits, body, jnp.zeros((1, t), I32))
    jmax = lax.cond(has_ties, tie_search, lambda: jnp.full((1, t), key_s.shape[0] * t, I32))

    def mask_body(ki, c):
        k = key_s[ki]
        pos = srow + ki * t
        sel = (k > thr) | ((k == thr) & (pos <= jmax))
        madd_s[ki] = jnp.where(sel, 0.0, NEG).astype(F32)
        return c
    lax.fori_loop(0, nblk, mask_body, 0)

    for hh in range(DSA_HEADS):
        m_s[...] = jnp.full(m_s.shape, NEG, F32)
        l_s[...] = jnp.zeros(l_s.shape, F32)
        acc_s[...] = jnp.zeros(acc_s.shape, F32)
        qh = sqT_ref[0, 0, hh * DSA_LATENT:(hh + 1) * DSA_LATENT, :]

        def step(ki, bias):
            kn = kvn_ref[0, pl.ds(pl.multiple_of(ki * t, t), t), :]
            logits = _dot(kn, qh) + bias + madd_s[ki]
            _softmax_step(logits, kvT_ref[0, ki], m_s, l_s, acc_s)

        bfar = bfar_ref[hh]

        def far_body(ki, c):
            step(ki, bfar)
            return c
        lax.fori_loop(0, jnp.maximum(qi - 1, 0), far_body, 0)

        @pl.when(qi >= 1)
        def _():
            step(qi - 1, bs_ref[hh])
        step(qi, bd_ref[hh])

        oT = (acc_s[...] / l_s[...]).astype(BF16)
        o_ref[0, 0, hh * DSA_V:(hh + 1) * DSA_V, :] = _dot(wuvT_ref[hh], oT).astype(BF16)


def _dsa_attention(ikn, iqT, iwT, kvn, kvT, sqT, w_uv, bias_dsa):
    B, S, _ = kvn.shape
    nt, t = iqT.shape[1], iqT.shape[3]
    k_top = min(TOPK_MAX, S // 4)
    bd, bs, bfar = _rel_bias_tiles(bias_dsa, t)
    wuvT = jnp.swapaxes(w_uv, 1, 2).astype(BF16)
    smem = pl.BlockSpec(memory_space=pltpu.SMEM)
    full = lambda a: pl.BlockSpec(a.shape, lambda b, i: (0,) * a.ndim)
    return pl.pallas_call(
        functools.partial(_dsa_kernel, k_top=k_top),
        grid=(B, nt),
        in_specs=[smem,
                  pl.BlockSpec((1, S, IDX_DIM), lambda b, i: (b, 0, 0)),
                  pl.BlockSpec((1, 1, 512, t), lambda b, i: (b, i, 0, 0)),
                  pl.BlockSpec((1, 1, 16, t), lambda b, i: (b, i, 0, 0)),
                  pl.BlockSpec((1, S, DSA_LATENT), lambda b, i: (b, 0, 0)),
                  pl.BlockSpec((1, nt, DSA_LATENT, t), lambda b, i: (b, 0, 0, 0)),
                  pl.BlockSpec((1, 1, 512, t), lambda b, i: (b, i, 0, 0)),
                  full(bd), full(bs), full(wuvT)],
        out_specs=pl.BlockSpec((1, 1, 512, t), lambda b, i: (b, i, 0, 0)),
        out_shape=jax.ShapeDtypeStruct((B, nt, DSA_HEADS * DSA_V, t), BF16),
        scratch_shapes=[pltpu.VMEM((nt, t, t), I32), pltpu.VMEM((nt, t, t), F32),
                        pltpu.VMEM((1, t), F32), pltpu.VMEM((1, t), F32),
                        pltpu.VMEM((DSA_LATENT, t), F32)],
        compiler_params=_cparams(("parallel", "arbitrary")),
        name="dsa_attention",
    )(bfar, ikn, iqT, iwT, kvn, kvT, sqT, bd, bs, wuvT)


def _outproj_kernel(x_ref, odT_ref, osT_ref, wa_ref, wb_ref, ga_ref, g2_ref, sc_ref, sh_ref,
                    x1_ref, h2_ref):
    mix = _dot_tn(odT_ref[0, 0], wa_ref[...]) + _dot_tn(osT_ref[0, 0], wb_ref[...])
    x1 = x_ref[0] + ga_ref[0] * mix
    x1_ref[0] = x1
    ms = jnp.mean(x1 * x1, axis=-1, keepdims=True)
    h = (x1 * lax.rsqrt(ms + RMS_EPS)) * g2_ref[...]
    h2_ref[0] = h * (1.0 + sc_ref[0]) + sh_ref[0]


def _outproj(x, odT, osT, w_out, mod3, g_norm_ffn):
    B, S, D = x.shape
    nt, t = odT.shape[1], odT.shape[3]
    wb = w_out.astype(BF16)
    wa, wbb = wb[:512], wb[512:]
    full = lambda a: pl.BlockSpec(a.shape, lambda b, i: (0,) * a.ndim)
    modspec = lambda j: pl.BlockSpec((1, 1, D), lambda b, i: (b, 0, j))
    xspec = pl.BlockSpec((1, t, D), lambda b, i: (b, i, 0))
    tspec = pl.BlockSpec((1, 1, 512, t), lambda b, i: (b, i, 0, 0))
    return pl.pallas_call(
        _outproj_kernel, grid=(B, nt),
        in_specs=[xspec, tspec, tspec, full(wa), full(wbb), modspec(2),
                  pl.BlockSpec((1, D), lambda b, i: (0, 0)), modspec(4), modspec(3)],
        out_specs=[xspec, xspec],
        out_shape=[jax.ShapeDtypeStruct((B, S, D), F32), jax.ShapeDtypeStruct((B, S, D), F32)],
        compiler_params=_cparams(("parallel", "parallel")), name="outproj",
    )(x, odT, osT, wa, wbb, mod3, g_norm_ffn.reshape(1, D), mod3, mod3)


def _extract_top(x, pos, payload, n):
    big = jnp.int32(2 ** 30)
    vals, pays = [], []
    for _ in range(n):
        m = jnp.max(x, axis=0, keepdims=True)
        first = jnp.min(jnp.where(x == m, pos, big), axis=0, keepdims=True)
        hit = pos == first
        if payload is None:
            pays.append(first)
        else:
            pays.append(jnp.max(jnp.where(hit, payload, -1), axis=0, keepdims=True))
        vals.append(m)
        x = jnp.where(hit, -jnp.inf, x)
    return jnp.concatenate(vals, axis=0), jnp.concatenate(pays, axis=0)


_CAND_PAIRS = [(a, b) for a in range(PEER_TOPK) for b in range(PEER_TOPK // (a + 1))]


def _route_kernel(h2_ref, wqT_ref, keys_ref, idx_ref, g_ref, idx_s, g_s):
    tm = h2_ref.shape[0]
    hb = h2_ref[...].astype(BF16)
    qT = _dot_nt(wqT_ref[...], hb).astype(BF16)
    krow = lax.broadcasted_iota(I32, (PEER_NKEYS, tm), 0)
    npair = len(_CAND_PAIRS)
    npad = (-npair) % 8
    cpos = jnp.concatenate(
        [jnp.full((1, tm), a * PEER_TOPK + b, I32) for a, b in _CAND_PAIRS]
        + [jnp.full((npad, tm), 2 ** 20, I32)], axis=0)
    for hh in range(PEER_HEADS):
        tops = []
        for p in range(2):
            r0 = (hh * 2 + p) * PEER_HALF
            sc = _dot(keys_ref[hh, p], qT[r0:r0 + PEER_HALF, :])
            tops.append(_extract_top(sc, krow, None, PEER_TOPK))
        (s1, i1), (s2, i2) = tops
        cand = jnp.concatenate(
            [s1[a:a + 1] + s2[b:b + 1] for a, b in _CAND_PAIRS]
            + [jnp.full((npad, tm), -jnp.inf, F32)], axis=0)
        cidx = jnp.concatenate(
            [i1[a:a + 1] * PEER_NKEYS + i2[b:b + 1] for a, b in _CAND_PAIRS]
            + [jnp.zeros((npad, tm), I32)], axis=0)
        top, e_idx = _extract_top(cand, cpos, cidx, PEER_TOPK)
        ex = jnp.exp(top - top[0:1])
        g = ex / jnp.sum(ex, axis=0, keepdims=True)
        idx_s[hh * PEER_TOPK:(hh + 1) * PEER_TOPK, :] = e_idx
        g_s[hh * PEER_TOPK:(hh + 1) * PEER_TOPK, :] = g
    idx_ref[...] = idx_s[...].T
    g_ref[...] = g_s[...].T


def _route(h2, w_q, keys):
    T, D = h2.shape
    tm = min(PEER_ROUTE_TILE, T)
    wqT = w_q.T.astype(BF16)
    kb = keys.astype(BF16)
    return pl.pallas_call(
        _route_kernel, grid=(T // tm,),
        in_specs=[pl.BlockSpec((tm, D), lambda i: (i, 0)),
                  pl.BlockSpec(wqT.shape, lambda i: (0, 0)),
                  pl.BlockSpec(kb.shape, lambda i: (0, 0, 0, 0))],
        out_specs=[pl.BlockSpec((tm, PEER_ROWS), lambda i: (i, 0)),
                   pl.BlockSpec((tm, PEER_ROWS), lambda i: (i, 0))],
        out_shape=[jax.ShapeDtypeStruct((T, PEER_ROWS), I32),
                   jax.ShapeDtypeStruct((T, PEER_ROWS), F32)],
        scratch_shapes=[pltpu.VMEM((PEER_ROWS, tm), I32), pltpu.VMEM((PEER_ROWS, tm), F32)],
        compiler_params=_cparams(("parallel",)), name="peer_route",
    )(h2, wqT, kb)


def _pack_rows(tab):
    n, d = tab.shape
    u = lax.bitcast_convert_type(tab.astype(BF16), jnp.uint16).astype(jnp.uint32)
    u = u.reshape(n, d // 256, 2, 128)
    w = u[:, :, 0, :] | (u[:, :, 1, :] << 16)
    return lax.bitcast_convert_type(w, I32)


def _unpack_words(w):
    lo = lax.bitcast_convert_type(w << 16, F32)
    hi = lax.bitcast_convert_type(w & jnp.int32(-65536), F32)
    return lo, hi


def _load_table_once(tab_hbm, tab_vmem, sem):
    @pl.when(pl.program_id(0) == 0)
    def _():
        cp = pltpu.make_async_copy(tab_hbm, tab_vmem, sem)
        cp.start()
        cp.wait()


def _expert_act_kernel(idx_ref, h2_ref, g_ref, tab_hbm, w_ref, tab, sem, rows):
    _load_table_once(tab_hbm, tab, sem)
    tn = h2_ref.shape[0]
    lane = lax.broadcasted_iota(I32, (1, 128), 1)

    def token(t, c):
        x = h2_ref[t]
        xe = jnp.concatenate([x[2 * j:2 * j + 1] for j in range(4)], axis=0)
        xo = jnp.concatenate([x[2 * j + 1:2 * j + 2] for j in range(4)], axis=0)
        for k in range(PEER_ROWS):
            lo, hi = _unpack_words(tab[idx_ref[t, k]])
            rows[k] = lo * xe + hi * xo
        part = jnp.sum(rows[...], axis=1)
        act = jnp.sum(part, axis=1, keepdims=True)
        eye = lax.broadcasted_iota(I32, (PEER_ROWS, 128), 0) == lax.broadcasted_iota(I32, (PEER_ROWS, 128), 1)
        act_row = jnp.sum(jnp.where(eye, act, 0.0), axis=0, keepdims=True)
        gelu = 0.5 * act_row * (1.0 + lax.erf(act_row * (2.0 ** -0.5)))
        w_ref[pl.ds(t, 1), :] = g_ref[pl.ds(t, 1), :] * gelu
        return c
    del lane
    lax.fori_loop(0, tn, token, 0)


def _expert_act(idx, h2r, g, u_pk):
    T = idx.shape[0]
    tn = min(PEER_TOK_TILE, T)
    return pl.pallas_call(
        _expert_act_kernel, grid=(T // tn,),
        in_specs=[pl.BlockSpec((tn, PEER_ROWS), lambda i: (i, 0), memory_space=pltpu.SMEM),
                  pl.BlockSpec((tn, 8, 128), lambda i: (i, 0, 0)),
                  pl.BlockSpec((tn, PEER_ROWS), lambda i: (i, 0)),
                  pl.BlockSpec(memory_space=pl.ANY)],
        out_specs=pl.BlockSpec((tn, PEER_ROWS), lambda i: (i, 0)),
        out_shape=jax.ShapeDtypeStruct((T, PEER_ROWS), F32),
        scratch_shapes=[pltpu.VMEM(u_pk.shape, I32), pltpu.SemaphoreType.DMA,
                        pltpu.VMEM((PEER_ROWS, 4, 128), F32)],
        compiler_params=_cparams(("arbitrary",)), name="peer_expert_act",
    )(idx, h2r, g, u_pk)


def _expert_out_kernel(idx_ref, w_ref, tab_hbm, o_ref, tab, sem):
    _load_table_once(tab_hbm, tab, sem)
    tn = o_ref.shape[0]

    def token(t, c):
        acc_lo = jnp.zeros((4, 128), F32)
        acc_hi = jnp.zeros((4, 128), F32)
        for k in range(PEER_ROWS):
            lo, hi = _unpack_words(tab[idx_ref[t, k]])
            wk = w_ref[t, k]
            acc_lo = acc_lo + wk * lo
            acc_hi = acc_hi + wk * hi
        o_ref[t, 0] = acc_lo
        o_ref[t, 1] = acc_hi
        return c
    lax.fori_loop(0, tn, token, 0)


def _expert_out(idx, w, v_pk):
    T = idx.shape[0]
    tn = min(PEER_TOK_TILE, T)
    return pl.pallas_call(
        _expert_out_kernel, grid=(T // tn,),
        in_specs=[pl.BlockSpec((tn, PEER_ROWS), lambda i: (i, 0), memory_space=pltpu.SMEM),
                  pl.BlockSpec((tn, PEER_ROWS), lambda i: (i, 0), memory_space=pltpu.SMEM),
                  pl.BlockSpec(memory_space=pl.ANY)],
        out_specs=pl.BlockSpec((tn, 2, 4, 128), lambda i: (i, 0, 0, 0)),
        out_shape=jax.ShapeDtypeStruct((T, 2, 4, 128), F32),
        scratch_shapes=[pltpu.VMEM(v_pk.shape, I32), pltpu.SemaphoreType.DMA],
        compiler_params=_cparams(("arbitrary",)), name="peer_expert_out",
    )(idx, w, v_pk)


def _final_kernel(x1_ref, p_ref, gf_ref, g_ref, o_ref, *, last_layer):
    x = x1_ref[0] + gf_ref[0] * p_ref[0]
    if last_layer:
        ms = jnp.mean(x * x, axis=-1, keepdims=True)
        x = (x * lax.rsqrt(ms + RMS_EPS)) * g_ref[...]
    o_ref[0] = x


def _final(x1, peer_out, mod3, g_final, last_layer):
    B, S, D = x1.shape
    tm = min(512, S)
    xspec = pl.BlockSpec((1, tm, D), lambda b, i: (b, i, 0))
    return pl.pallas_call(
        functools.partial(_final_kernel, last_layer=last_layer), grid=(B, S // tm),
        in_specs=[xspec, xspec, pl.BlockSpec((1, 1, D), lambda b, i: (b, 0, 5)),
                  pl.BlockSpec((1, D), lambda b, i: (0, 0))],
        out_specs=xspec,
        out_shape=jax.ShapeDtypeStruct((B, S, D), F32),
        compiler_params=_cparams(("parallel", "parallel")), name="final_norm",
    )(x1, peer_out, mod3, g_final.reshape(1, D))


def kernel(x, c, w_ada, b_ada, g_norm_mix, w_in, lam_q1, lam_k1, lam_q2, lam_k2, g_subln, g_kv_norm,
           w_uv, w_out, g_norm_ffn, w_peer_q, peer_keys, peer_u, peer_v, rel_bias, g_final):
    B, S, D = x.shape
    depth = w_ada.shape[0]
    bias_diff = rel_bias[:, :DIFF_HEADS]
    bias_dsa = rel_bias[:, DIFF_HEADS:]
    for l in range(depth):
        mod3 = _adaln(c, w_ada[l], b_ada[l]).reshape(B, 1, 6 * D)
        lambda_init = 0.8 - 0.6 * math.exp(-0.3 * l)
        lam = (jnp.exp(jnp.sum(lam_q1[l] * lam_k1[l])) - jnp.exp(jnp.sum(lam_q2[l] * lam_k2[l]))
               + lambda_init).astype(F32)

        dk, kvn, ikn, dqT, dvT, sqT, kvT, iqT, iwT = _inproj(x, mod3, g_norm_mix[l], w_in[l], g_kv_norm[l])
        odT = _diff_attention(dk, dqT, dvT, lam, bias_diff, g_subln[l], lambda_init)
        osT = _dsa_attention(ikn, iqT, iwT, kvn, kvT, sqT, w_uv[l], bias_dsa)
        x1, h2 = _outproj(x, odT, osT, w_out[l], mod3, g_norm_ffn[l])

        T = B * S
        h2f = h2.reshape(T, D)
        idx, g = _route(h2f, w_peer_q[l], peer_keys[l])
        w = _expert_act(idx, h2f.reshape(T, 8, 128), g, _pack_rows(peer_u[l]))
        po = _expert_out(idx, w, _pack_rows(peer_v[l]))
        peer_out = jnp.swapaxes(po, 1, 2).reshape(B, S, D)
        x = _final(x1, peer_out, mod3, g_final, last_layer=(l == depth - 1))
    return x
```

```python
import functools
import math

import jax
import jax.numpy as jnp
import numpy as np
from jax import lax
from jax.experimental import pallas as pl
from jax.experimental.pallas import tpu as pltpu

F32 = jnp.float32
BF16 = jnp.bfloat16
I32 = jnp.int32

DIFF_HEADS = 4
DIFF_QK = 64
DIFF_V = 128
DSA_HEADS = 4
DSA_LATENT = 128
DSA_V = 128
IDX_HEADS = 8
IDX_DIM = 64
TOPK_MAX = 256
REL_BUCKETS = 32
REL_MAX_DIST = 128
PEER_HEADS = 8
PEER_NKEYS = 128
PEER_HALF = 64
PEER_TOPK = 16
RMS_EPS = 1e-6

NEG = -1e30
INT_MIN = -(2 ** 31)
VMEM_LIMIT = 56 * 1024 * 1024

ATT_TILE = 256
PEER_ROUTE_TILE = 256
PEER_ROWS = PEER_HEADS * PEER_TOPK


def _cparams(sem):
    return pltpu.CompilerParams(dimension_semantics=sem, vmem_limit_bytes=VMEM_LIMIT)


def _dot(a, b):
    return jnp.dot(a, b, preferred_element_type=F32)


def _dot_nt(a, b):
    return lax.dot_general(a, b, (((1,), (1,)), ((), ())), preferred_element_type=F32)


def _dot_tn(a, b):
    return lax.dot_general(a, b, (((0,), (0,)), ((), ())), preferred_element_type=F32)


def _adaln_kernel(c_ref, w_ref, b_ref, o_ref):
    c = c_ref[...]
    ca = c * (1.0 / (1.0 + jnp.exp(-c)))
    o_ref[...] = _dot(ca, w_ref[...]) + b_ref[...]


def _adaln(c, w, b):
    B, D = c.shape
    N = w.shape[1]
    tn = 1024
    return pl.pallas_call(
        _adaln_kernel,
        grid=(N // tn,),
        in_specs=[pl.BlockSpec((B, D), lambda j: (0, 0)),
                  pl.BlockSpec((D, tn), lambda j: (0, j)),
                  pl.BlockSpec((1, tn), lambda j: (0, j))],
        out_specs=pl.BlockSpec((B, tn), lambda j: (0, j)),
        out_shape=jax.ShapeDtypeStruct((B, N), F32),
        compiler_params=_cparams(("arbitrary",)),
        name="adaln",
    )(c, w, b.reshape(1, N))


def _inproj_kernel(x_ref, sc_ref, sh_ref, g_ref,
                   wn_dk, wn_kv, wn_ik, wt_dq, wt_dv, wt_sq, wt_kv, wt_iq, wt_iw,
                   gkv_row, gkv_col,
                   dk_o, kvn_o, ik_o, dqT_o, dvT_o, sqT_o, kvT_o, iqT_o, iwT_o):
    x = x_ref[0]
    ms = jnp.mean(x * x, axis=-1, keepdims=True)
    h = (x * lax.rsqrt(ms + RMS_EPS)) * g_ref[...]
    h = h * (1.0 + sc_ref[0]) + sh_ref[0]
    hb = h.astype(BF16)

    dk_o[0] = _dot(hb, wn_dk[...]).astype(BF16)
    kv = _dot(hb, wn_kv[...])
    kv = kv * lax.rsqrt(jnp.mean(kv * kv, axis=-1, keepdims=True) + RMS_EPS) * gkv_row[...]
    kvn_o[0] = kv.astype(BF16)
    ik_o[0] = _dot(hb, wn_ik[...]).astype(BF16)

    dqT_o[0, 0] = (_dot_nt(wt_dq[...], hb) * (DIFF_QK ** -0.5)).astype(BF16)
    dvT_o[0, 0] = _dot_nt(wt_dv[...], hb).astype(BF16)
    sqT_o[0, 0] = (_dot_nt(wt_sq[...], hb) * (DSA_LATENT ** -0.5)).astype(BF16)
    kvT = _dot_nt(wt_kv[...], hb)
    kvT = kvT * lax.rsqrt(jnp.mean(kvT * kvT, axis=0, keepdims=True) + RMS_EPS) * gkv_col[...]
    kvT_o[0, 0] = kvT.astype(BF16)
    iqT_o[0, 0] = (_dot_nt(wt_iq[...], hb) * (IDX_DIM ** -0.5)).astype(BF16)
    iwT_o[0, 0] = _dot_nt(wt_iw[...], hb) * (IDX_HEADS ** -0.5)


def _inproj(x, mod3, g_norm, w_in, g_kv):
    B, S, D = x.shape
    tm = ATT_TILE
    nt = S // tm
    sizes = (512, 512, 512, 512, 128, 512, 64, 8)
    offs = np.cumsum((0,) + sizes)
    wb = w_in.astype(BF16)
    piece = lambda i: wb[:, offs[i]:offs[i + 1]]
    w_dq, w_dk, w_dv, w_sq, w_kv, w_iq, w_ik, w_iw = [piece(i) for i in range(8)]
    w_iwT = jnp.zeros((16, D), BF16).at[:8].set(w_iw.T)

    full = lambda a: pl.BlockSpec(a.shape, lambda b, i: (0,) * a.ndim)
    nat = lambda f: pl.BlockSpec((1, tm, f), lambda b, i: (b, i, 0))
    tr = lambda f: pl.BlockSpec((1, 1, f, tm), lambda b, i: (b, i, 0, 0))
    ins = [x, mod3, mod3, g_norm.reshape(1, D),
           w_dk, w_kv, w_ik, w_dq.T, w_dv.T, w_sq.T, w_kv.T, w_iq.T, w_iwT,
           g_kv.reshape(1, -1), g_kv.reshape(-1, 1)]
    in_specs = [pl.BlockSpec((1, tm, D), lambda b, i: (b, i, 0)),
                pl.BlockSpec((1, 1, D), lambda b, i: (b, 0, 1)),
                pl.BlockSpec((1, 1, D), lambda b, i: (b, 0, 0)),
                ] + [full(a) for a in ins[3:]]
    out_shape = [jax.ShapeDtypeStruct((B, S, 512), BF16),
                 jax.ShapeDtypeStruct((B, S, 128), BF16),
                 jax.ShapeDtypeStruct((B, S, 64), BF16),
                 jax.ShapeDtypeStruct((B, nt, 512, tm), BF16),
                 jax.ShapeDtypeStruct((B, nt, 512, tm), BF16),
                 jax.ShapeDtypeStruct((B, nt, 512, tm), BF16),
                 jax.ShapeDtypeStruct((B, nt, 128, tm), BF16),
                 jax.ShapeDtypeStruct((B, nt, 512, tm), BF16),
                 jax.ShapeDtypeStruct((B, nt, 16, tm), F32)]
    out_specs = [nat(512), nat(128), nat(64), tr(512), tr(512), tr(512), tr(128), tr(512), tr(16)]
    return pl.pallas_call(
        _inproj_kernel, grid=(B, nt), in_specs=in_specs, out_specs=out_specs, out_shape=out_shape,
        compiler_params=_cparams(("parallel", "parallel")), name="inproj",
    )(*ins)


def _softmax_step(logits, vT, m_ref, l_ref, acc_ref):
    m_prev = m_ref[...]
    m_new = jnp.maximum(m_prev, jnp.max(logits, axis=0, keepdims=True))
    alpha = jnp.exp(m_prev - m_new)
    p = jnp.exp(logits - m_new)
    l_ref[...] = alpha * l_ref[...] + jnp.sum(p, axis=0, keepdims=True)
    acc_ref[...] = alpha * acc_ref[...] + _dot(vT, p.astype(BF16))
    m_ref[...] = m_new


def _rel_bias_tiles(rel_bias_h, t):
    def bucket(dist):
        n = jnp.maximum(dist, 0)
        max_exact = REL_BUCKETS // 2
        nf = jnp.maximum(n, max_exact).astype(F32)
        large = max_exact + (jnp.log(nf / max_exact) / math.log(REL_MAX_DIST / max_exact)
                             * (REL_BUCKETS - max_exact)).astype(I32)
        large = jnp.minimum(large, REL_BUCKETS - 1)
        return jnp.where(n < max_exact, n, large)
    s = jnp.arange(t, dtype=I32)[:, None]
    q = jnp.arange(t, dtype=I32)[None, :]
    d_diag = q - s
    d_sub = q - s + t
    tb = rel_bias_h.astype(F32).T

    def lookup(bk):
        out = jnp.zeros((tb.shape[0],) + bk.shape, F32)
        for k in range(REL_BUCKETS):
            out = jnp.where(bk[None] == k, tb[:, k][:, None, None], out)
        return out
    diag = jnp.where(d_diag >= 0, lookup(bucket(d_diag)), NEG)
    sub = lookup(bucket(d_sub))
    far = tb[:, REL_BUCKETS - 1]
    assert t >= REL_MAX_DIST
    return diag, sub, far


def _diff_kernel(lam_ref, bfar_ref, kn_ref, qT_ref, vT_ref, bd_ref, bs_ref, gsub_ref, o_ref,
                 m1, l1, a1, m2, l2, a2, *, lambda_init):
    h = pl.program_id(1)
    qi = pl.program_id(2)
    tq = qT_ref.shape[-1]
    tk = tq
    qT = qT_ref[0, 0]
    row = lax.broadcasted_iota(I32, qT.shape, 0)
    zero = jnp.zeros_like(qT)
    q1 = jnp.where(row < DIFF_QK, qT, zero)
    q2 = jnp.where(row >= DIFF_QK, qT, zero)
    for m_, l_, a_ in ((m1, l1, a1), (m2, l2, a2)):
        m_[...] = jnp.full(m_.shape, NEG, F32)
        l_[...] = jnp.zeros(l_.shape, F32)
        a_[...] = jnp.zeros(a_.shape, F32)

    def step(ki, bias):
        kn = kn_ref[0, pl.ds(pl.multiple_of(ki * tk, tk), tk), :]
        vT = vT_ref[0, ki]
        _softmax_step(_dot(kn, q1) + bias, vT, m1, l1, a1)
        _softmax_step(_dot(kn, q2) + bias, vT, m2, l2, a2)

    bfar = bfar_ref[h]

    def far_body(ki, c):
        step(ki, bfar)
        return c
    lax.fori_loop(0, jnp.maximum(qi - 1, 0), far_body, 0)

    @pl.when(qi >= 1)
    def _():
        step(qi - 1, bs_ref[0])
    step(qi, bd_ref[0])

    lam = lam_ref[0]
    o = a1[...] / l1[...] - lam * (a2[...] / l2[...])
    o = o * lax.rsqrt(jnp.mean(o * o, axis=0, keepdims=True) + RMS_EPS) * gsub_ref[...]
    o_ref[0, 0] = (o * (1.0 - lambda_init)).astype(BF16)


def _diff_attention(dk, dqT, dvT, lam, bias_diff, g_subln, lambda_init):
    B, S, _ = dk.shape
    nt, t = dqT.shape[1], dqT.shape[3]
    H = DIFF_HEADS
    bd, bs, bfar = _rel_bias_tiles(bias_diff, t)
    smem = pl.BlockSpec(memory_space=pltpu.SMEM)
    return pl.pallas_call(
        functools.partial(_diff_kernel, lambda_init=lambda_init),
        grid=(B, H, nt),
        in_specs=[smem, smem,
                  pl.BlockSpec((1, S, 128), lambda b, h, i: (b, 0, h)),
                  pl.BlockSpec((1, 1, 128, t), lambda b, h, i: (b, i, h, 0)),
                  pl.BlockSpec((1, nt, 128, t), lambda b, h, i: (b, 0, h, 0)),
                  pl.BlockSpec((1, t, t), lambda b, h, i: (h, 0, 0)),
                  pl.BlockSpec((1, t, t), lambda b, h, i: (h, 0, 0)),
                  pl.BlockSpec((DIFF_V, 1), lambda b, h, i: (0, 0))],
        out_specs=pl.BlockSpec((1, 1, 128, t), lambda b, h, i: (b, i, h, 0)),
        out_shape=jax.ShapeDtypeStruct((B, nt, H * DIFF_V, t), BF16),
        scratch_shapes=[pltpu.VMEM((1, t), F32), pltpu.VMEM((1, t), F32), pltpu.VMEM((DIFF_V, t), F32),
                        pltpu.VMEM((1, t), F32), pltpu.VMEM((1, t), F32), pltpu.VMEM((DIFF_V, t), F32)],
        compiler_params=_cparams(("parallel", "parallel", "arbitrary")),
        name="diff_attention",
    )(lam.reshape(1), bfar, dk, dqT, dvT, bd, bs, g_subln.reshape(-1, 1).astype(F32))


def _dsa_kernel(bfar_ref, ikn_ref, iqT_ref, iwT_ref, kvn_ref, kvT_ref, sqT_ref, bd_ref, bs_ref,
                wuvT_ref, o_ref, key_s, madd_s, m_s, l_s, acc_s, *, k_top):
    qi = pl.program_id(1)
    t = iqT_ref.shape[-1]
    nblk = qi + 1

    iqT = iqT_ref[0, 0]
    iwT = iwT_ref[0, 0]
    srow = lax.broadcasted_iota(I32, (t, t), 0)
    tcol = lax.broadcasted_iota(I32, (t, t), 1)

    def score_body(ki, c):
        ik = ikn_ref[0, pl.ds(pl.multiple_of(ki * t, t), t), :]
        sc = jnp.zeros((t, t), F32)
        for hh in range(IDX_HEADS):
            d = _dot(ik, iqT[hh * IDX_DIM:(hh + 1) * IDX_DIM, :])
            sc = sc + jnp.maximum(d, 0.0) * iwT[hh:hh + 1, :]
        sc = jnp.where((ki < qi) | (srow <= tcol), sc, -jnp.inf)
        bits = lax.bitcast_convert_type(sc, I32)
        key_s[ki] = bits ^ ((bits >> 31) & jnp.int32(0x7FFFFFFF))
        return c
    lax.fori_loop(0, nblk, score_body, 0)

    def count_ge(cand):
        def body(ki, acc):
            ge = (key_s[ki] >= cand).astype(I32)
            return acc + jnp.sum(ge.reshape(t // 8, 8, t), axis=0)
        part = lax.fori_loop(0, nblk, body, jnp.zeros((8, t), I32))
        return jnp.sum(part, axis=0, keepdims=True)

    zero_t = jnp.zeros((1, t), I32)
    thr = jnp.where(count_ge(zero_t) >= k_top, zero_t, jnp.full((1, t), INT_MIN, I32))

    def bit_body(i, thr):
        cand = thr + (jnp.int32(1) << (30 - i))
        return jnp.where(count_ge(cand) >= k_top, cand, thr)
    thr = lax.fori_loop(0, 31, bit_body, thr)

    def count_gt_eq(ki, acc):
        k = key_s[ki]
        gt = (k > thr).astype(I32)
        eq = (k == thr).astype(I32)
        return (acc[0] + jnp.sum(gt.reshape(t // 8, 8, t), axis=0),
                acc[1] + jnp.sum(eq.reshape(t // 8, 8, t), axis=0))
    z8 = jnp.zeros((8, t), I32)
    gt8, eq8 = lax.fori_loop(0, nblk, count_gt_eq, (z8, z8))
    need = k_top - jnp.sum(gt8, axis=0, keepdims=True)
    n_eq = jnp.sum(eq8, axis=0, keepdims=True)
    has_ties = jnp.max(jnp.where(n_eq > need, 1, 0)) > 0

    def count_eq_le(j):
        def body(ki, acc):
            pos = srow + ki * t
            hit = ((key_s[ki] == thr) & (pos <= j)).astype(I32)
            return acc + jnp.sum(hit.reshape(t // 8, 8, t), axis=0)
        part = lax.fori_loop(0, nblk, body, jnp.zeros((8, t), I32))
        return jnp.sum(part, axis=0, keepdims=True)

    def tie_search():
        nbits = max(1, int(math.ceil(math.log2(key_s.shape[0] * t))))

        def body(i, lo):
            cand = lo + (jnp.int32(1) << (nbits - 1 - i))
            return jnp.where(count_eq_le(cand - 1) < need, cand, lo)
        return lax.fori_loop(0, nbits, body, jnp.zeros((1, t), I32))
    jmax = lax.cond(has_ties, tie_search, lambda: jnp.full((1, t), key_s.shape[0] * t, I32))

    def mask_body(ki, c):
        k = key_s[ki]
        pos = srow + ki * t
        sel = (k > thr) | ((k == thr) & (pos <= jmax))
        madd_s[ki] = jnp.where(sel, 0.0, NEG).astype(F32)
        return c
    lax.fori_loop(0, nblk, mask_body, 0)

    for hh in range(DSA_HEADS):
        m_s[...] = jnp.full(m_s.shape, NEG, F32)
        l_s[...] = jnp.zeros(l_s.shape, F32)
        acc_s[...] = jnp.zeros(acc_s.shape, F32)
        qh = sqT_ref[0, 0, hh * DSA_LATENT:(hh + 1) * DSA_LATENT, :]

        def step(ki, bias):
            kn = kvn_ref[0, pl.ds(pl.multiple_of(ki * t, t), t), :]
            logits = _dot(kn, qh) + bias + madd_s[ki]
            _softmax_step(logits, kvT_ref[0, ki], m_s, l_s, acc_s)

        bfar = bfar_ref[hh]

        def far_body(ki, c):
            step(ki, bfar)
            return c
        lax.fori_loop(0, jnp.maximum(qi - 1, 0), far_body, 0)

        @pl.when(qi >= 1)
        def _():
            step(qi - 1, bs_ref[hh])
        step(qi, bd_ref[hh])

        oT = (acc_s[...] / l_s[...]).astype(BF16)
        o_ref[0, 0, hh * DSA_V:(hh + 1) * DSA_V, :] = _dot(wuvT_ref[hh], oT).astype(BF16)


def _dsa_attention(ikn, iqT, iwT, kvn, kvT, sqT, w_uv, bias_dsa):
    B, S, _ = kvn.shape
    nt, t = iqT.shape[1], iqT.shape[3]
    k_top = min(TOPK_MAX, S // 4)
    bd, bs, bfar = _rel_bias_tiles(bias_dsa, t)
    wuvT = jnp.swapaxes(w_uv, 1, 2).astype(BF16)
    smem = pl.BlockSpec(memory_space=pltpu.SMEM)
    full = lambda a: pl.BlockSpec(a.shape, lambda b, i: (0,) * a.ndim)
    return pl.pallas_call(
        functools.partial(_dsa_kernel, k_top=k_top),
        grid=(B, nt),
        in_specs=[smem,
                  pl.BlockSpec((1, S, IDX_DIM), lambda b, i: (b, 0, 0)),
                  pl.BlockSpec((1, 1, 512, t), lambda b, i: (b, i, 0, 0)),
                  pl.BlockSpec((1, 1, 16, t), lambda b, i: (b, i, 0, 0)),
                  pl.BlockSpec((1, S, DSA_LATENT), lambda b, i: (b, 0, 0)),
                  pl.BlockSpec((1, nt, DSA_LATENT, t), lambda b, i: (b, 0, 0, 0)),
                  pl.BlockSpec((1, 1, 512, t), lambda b, i: (b, i, 0, 0)),
                  full(bd), full(bs), full(wuvT)],
        out_specs=pl.BlockSpec((1, 1, 512, t), lambda b, i: (b, i, 0, 0)),
        out_shape=jax.ShapeDtypeStruct((B, nt, DSA_HEADS * DSA_V, t), BF16),
        scratch_shapes=[pltpu.VMEM((nt, t, t), I32), pltpu.VMEM((nt, t, t), F32),
                        pltpu.VMEM((1, t), F32), pltpu.VMEM((1, t), F32),
                        pltpu.VMEM((DSA_LATENT, t), F32)],
        compiler_params=_cparams(("parallel", "arbitrary")),
        name="dsa_attention",
    )(bfar, ikn, iqT, iwT, kvn, kvT, sqT, bd, bs, wuvT)


def _outproj_kernel(x_ref, odT_ref, osT_ref, wa_ref, wb_ref, ga_ref, g2_ref, sc_ref, sh_ref,
                    x1_ref, h2_ref):
    mix = _dot_tn(odT_ref[0, 0], wa_ref[...]) + _dot_tn(osT_ref[0, 0], wb_ref[...])
    x1 = x_ref[0] + ga_ref[0] * mix
    x1_ref[0] = x1
    ms = jnp.mean(x1 * x1, axis=-1, keepdims=True)
    h = (x1 * lax.rsqrt(ms + RMS_EPS)) * g2_ref[...]
    h2_ref[0] = h * (1.0 + sc_ref[0]) + sh_ref[0]


def _outproj(x, odT, osT, w_out, mod3, g_norm_ffn):
    B, S, D = x.shape
    nt, t = odT.shape[1], odT.shape[3]
    wb = w_out.astype(BF16)
    wa, wbb = wb[:512], wb[512:]
    full = lambda a: pl.BlockSpec(a.shape, lambda b, i: (0,) * a.ndim)
    modspec = lambda j: pl.BlockSpec((1, 1, D), lambda b, i: (b, 0, j))
    xspec = pl.BlockSpec((1, t, D), lambda b, i: (b, i, 0))
    tspec = pl.BlockSpec((1, 1, 512, t), lambda b, i: (b, i, 0, 0))
    return pl.pallas_call(
        _outproj_kernel, grid=(B, nt),
        in_specs=[xspec, tspec, tspec, full(wa), full(wbb), modspec(2),
                  pl.BlockSpec((1, D), lambda b, i: (0, 0)), modspec(4), modspec(3)],
        out_specs=[xspec, xspec],
        out_shape=[jax.ShapeDtypeStruct((B, S, D), F32), jax.ShapeDtypeStruct((B, S, D), F32)],
        compiler_params=_cparams(("parallel", "parallel")), name="outproj",
    )(x, odT, osT, wa, wbb, mod3, g_norm_ffn.reshape(1, D), mod3, mod3)


def _extract_top(x, pos, payload, n):
    big = jnp.int32(2 ** 30)
    vals, pays = [], []
    for _ in range(n):
        m = jnp.max(x, axis=0, keepdims=True)
        first = jnp.min(jnp.where(x == m, pos, big), axis=0, keepdims=True)
        hit = pos == first
        if payload is None:
            pays.append(first)
        else:
            pays.append(jnp.max(jnp.where(hit, payload, -1), axis=0, keepdims=True))
        vals.append(m)
        x = jnp.where(hit, -jnp.inf, x)
    return jnp.concatenate(vals, axis=0), jnp.concatenate(pays, axis=0)


_CAND_PAIRS = [(a, b) for a in range(PEER_TOPK) for b in range(PEER_TOPK // (a + 1))]


def _route_kernel(h2_ref, wqT_ref, keys_ref, idx_ref, g_ref, idx_s, g_s):
    tm = h2_ref.shape[0]
    hb = h2_ref[...].astype(BF16)
    qT = _dot_nt(wqT_ref[...], hb).astype(BF16)
    krow = lax.broadcasted_iota(I32, (PEER_NKEYS, tm), 0)
    npair = len(_CAND_PAIRS)
    npad = (-npair) % 8
    cpos = jnp.concatenate(
        [jnp.full((1, tm), a * PEER_TOPK + b, I32) for a, b in _CAND_PAIRS]
        + [jnp.full((npad, tm), 2 ** 20, I32)], axis=0)
    for hh in range(PEER_HEADS):
        tops = []
        for p in range(2):
            r0 = (hh * 2 + p) * PEER_HALF
            sc = _dot(keys_ref[hh, p], qT[r0:r0 + PEER_HALF, :])
            tops.append(_extract_top(sc, krow, None, PEER_TOPK))
        (s1, i1), (s2, i2) = tops
        cand = jnp.concatenate(
            [s1[a:a + 1] + s2[b:b + 1] for a, b in _CAND_PAIRS]
            + [jnp.full((npad, tm), -jnp.inf, F32)], axis=0)
        cidx = jnp.concatenate(
            [i1[a:a + 1] * PEER_NKEYS + i2[b:b + 1] for a, b in _CAND_PAIRS]
            + [jnp.zeros((npad, tm), I32)], axis=0)
        top, e_idx = _extract_top(cand, cpos, cidx, PEER_TOPK)
        ex = jnp.exp(top - top[0:1])
        g = ex / jnp.sum(ex, axis=0, keepdims=True)
        idx_s[hh * PEER_TOPK:(hh + 1) * PEER_TOPK, :] = e_idx
        g_s[hh * PEER_TOPK:(hh + 1) * PEER_TOPK, :] = g
    idx_ref[...] = (idx_s[...] * PACK_ROWS).T
    g_ref[...] = g_s[...].T


def _route(h2, w_q, keys):
    T, D = h2.shape
    tm = min(PEER_ROUTE_TILE, T)
    wqT = w_q.T.astype(BF16)
    kb = keys.astype(BF16)
    return pl.pallas_call(
        _route_kernel, grid=(T // tm,),
        in_specs=[pl.BlockSpec((tm, D), lambda i: (i, 0)),
                  pl.BlockSpec(wqT.shape, lambda i: (0, 0)),
                  pl.BlockSpec(kb.shape, lambda i: (0, 0, 0, 0))],
        out_specs=[pl.BlockSpec((tm, PEER_ROWS), lambda i: (i, 0)),
                   pl.BlockSpec((tm, PEER_ROWS), lambda i: (i, 0))],
        out_shape=[jax.ShapeDtypeStruct((T, PEER_ROWS), I32),
                   jax.ShapeDtypeStruct((T, PEER_ROWS), F32)],
        scratch_shapes=[pltpu.VMEM((PEER_ROWS, tm), I32), pltpu.VMEM((PEER_ROWS, tm), F32)],
        compiler_params=_cparams(("parallel",)), name="peer_route",
    )(h2, wqT, kb)


PACK_ROWS = 4


def _pack_rows(tab):
    n, d = tab.shape
    assert d == 2 * PACK_ROWS * 128
    u = lax.bitcast_convert_type(tab.astype(BF16), jnp.uint16).astype(jnp.uint32)
    u = u.reshape(n, 2, PACK_ROWS, 128)
    w = u[:, 0] | (u[:, 1] << 16)
    return lax.bitcast_convert_type(w, I32).reshape(n * PACK_ROWS, 128)


def _load_table_once(tab_hbm, tab_vmem, sem):
    @pl.when(pl.program_id(0) == 0)
    def _():
        cp = pltpu.make_async_copy(tab_hbm, tab_vmem, sem)
        cp.start()
        cp.wait()


EXPERT_TOK_TILE = 64
EXPERT_GROUP = 4


def _expert_masks():
    r = np.arange(8 * PEER_ROWS) % 8
    dmask = (np.arange(8)[:, None] == (4 * (r % 2) + r // 2)[None, :]).astype(np.float32)
    ssum = (np.arange(8 * PEER_ROWS)[:, None] // 8 == np.arange(PEER_ROWS)[None, :]).astype(np.float32)
    return jnp.asarray(dmask), jnp.asarray(ssum, BF16)


def _gather_compute_pipeline(idx_ref, tab, bufs, ngroups, compute):
    def gather(g, buf):
        for u in range(EXPERT_GROUP):
            t = g * EXPERT_GROUP + u
            for k in range(PEER_ROWS):
                off = pl.multiple_of(idx_ref[t, k], PACK_ROWS)
                buf[u, PACK_ROWS * k:PACK_ROWS * (k + 1), :] = tab[pl.ds(off, PACK_ROWS), :]

    gather(0, bufs[0])

    def body(j, c):
        g0 = 2 * j
        gather(g0 + 1, bufs[1])
        compute(g0, bufs[0])
        gather(jnp.minimum(g0 + 2, ngroups - 1), bufs[0])
        compute(g0 + 1, bufs[1])
        return c
    lax.fori_loop(0, ngroups // 2, body, 0)


def _expert_act_kernel(idx_ref, h2_ref, g_ref, dmask_ref, ssum_ref, tab_hbm, w_ref,
                       tab, sem, buf0, buf1, zs):
    _load_table_once(tab_hbm, tab, sem)
    tn = h2_ref.shape[0]
    dmask = dmask_ref[...]

    def compute(g, buf):
        for u in range(EXPERT_GROUP):
            t = g * EXPERT_GROUP + u
            rows = pltpu.bitcast(buf[u], BF16)
            y = _dot_nt(h2_ref[t].astype(BF16), rows)
            zs[pl.ds(pl.multiple_of(t * 8, 8), 8), :] = y * dmask

    _gather_compute_pipeline(idx_ref, tab, (buf0, buf1), tn // EXPERT_GROUP, compute)

    z = zs[...]
    z_hi = z.astype(BF16)
    z_lo = (z - z_hi.astype(F32)).astype(BF16)
    s = _dot(z_hi, ssum_ref[...]) + _dot(z_lo, ssum_ref[...])
    act = jnp.sum(s.reshape(tn, 8, PEER_ROWS), axis=1)
    gelu = 0.5 * act * (1.0 + lax.erf(act * (2.0 ** -0.5)))
    w_ref[...] = g_ref[...] * gelu


def _expert_out_kernel(idx_ref, w_ref, dmask_ref, ssum_ref, tab_hbm, o_ref,
                       tab, sem, buf0, buf1, wrep):
    _load_table_once(tab_hbm, tab, sem)
    tn = o_ref.shape[0]
    dmask = dmask_ref[...]
    wrep[...] = _dot_nt(w_ref[...].astype(BF16), ssum_ref[...])

    def compute(g, buf):
        for u in range(EXPERT_GROUP):
            t = g * EXPERT_GROUP + u
            rows = pltpu.bitcast(buf[u], BF16)
            a = (jnp.broadcast_to(wrep[pl.ds(t, 1), :], dmask.shape) * dmask).astype(BF16)
            o_ref[t] = _dot(a, rows)

    _gather_compute_pipeline(idx_ref, tab, (buf0, buf1), tn // EXPERT_GROUP, compute)


def _expert_pass(body, idx, operands, operand_specs, tab_pk, out_shape, out_spec, extra_scratch, name):
    T = idx.shape[0]
    tn = min(EXPERT_TOK_TILE, T)
    assert T % tn == 0 and (tn // EXPERT_GROUP) % 2 == 0
    dmask, ssum = _expert_masks()
    buf = pltpu.VMEM((EXPERT_GROUP, PACK_ROWS * PEER_ROWS, 128), I32)
    return pl.pallas_call(
        body, grid=(T // tn,),
        in_specs=[pl.BlockSpec((tn, PEER_ROWS), lambda i: (i, 0), memory_space=pltpu.SMEM)]
        + operand_specs(tn)
        + [pl.BlockSpec(dmask.shape, lambda i: (0, 0)), pl.BlockSpec(ssum.shape, lambda i: (0, 0)),
           pl.BlockSpec(memory_space=pl.ANY)],
        out_specs=out_spec(tn), out_shape=out_shape,
        scratch_shapes=[pltpu.VMEM(tab_pk.shape, I32), pltpu.SemaphoreType.DMA, buf, buf]
        + extra_scratch(tn),
        compiler_params=_cparams(("arbitrary",)), name=name,
    )(idx, *operands, dmask, ssum, tab_pk)


def _expert_act(idx, h2r, g, u_pk):
    T = idx.shape[0]
    row = lambda tn: pl.BlockSpec((tn, PEER_ROWS), lambda i: (i, 0))
    return _expert_pass(
        _expert_act_kernel, idx, (h2r, g),
        lambda tn: [pl.BlockSpec((tn, 8, 128), lambda i: (i, 0, 0)), row(tn)], u_pk,
        jax.ShapeDtypeStruct((T, PEER_ROWS), F32), row,
        lambda tn: [pltpu.VMEM((tn * 8, 8 * PEER_ROWS), F32)], "peer_expert_act")


def _expert_out(idx, w, v_pk):
    T = idx.shape[0]
    return _expert_pass(
        _expert_out_kernel, idx, (w,),
        lambda tn: [pl.BlockSpec((tn, PEER_ROWS), lambda i: (i, 0))], v_pk,
        jax.ShapeDtypeStruct((T, 8, 128), F32),
        lambda tn: pl.BlockSpec((tn, 8, 128), lambda i: (i, 0, 0)),
        lambda tn: [pltpu.VMEM((tn, 8 * PEER_ROWS), F32)], "peer_expert_out")


def _final_kernel(x1_ref, p_ref, gf_ref, g_ref, o_ref, *, last_layer):
    x = x1_ref[0] + gf_ref[0] * p_ref[0]
    if last_layer:
        ms = jnp.mean(x * x, axis=-1, keepdims=True)
        x = (x * lax.rsqrt(ms + RMS_EPS)) * g_ref[...]
    o_ref[0] = x


def _final(x1, peer_out, mod3, g_final, last_layer):
    B, S, D = x1.shape
    tm = min(512, S)
    xspec = pl.BlockSpec((1, tm, D), lambda b, i: (b, i, 0))
    return pl.pallas_call(
        functools.partial(_final_kernel, last_layer=last_layer), grid=(B, S // tm),
        in_specs=[xspec, xspec, pl.BlockSpec((1, 1, D), lambda b, i: (b, 0, 5)),
                  pl.BlockSpec((1, D), lambda b, i: (0, 0))],
        out_specs=xspec,
        out_shape=jax.ShapeDtypeStruct((B, S, D), F32),
        compiler_params=_cparams(("parallel", "parallel")), name="final_norm",
    )(x1, peer_out, mod3, g_final.reshape(1, D))


def kernel(x, c, w_ada, b_ada, g_norm_mix, w_in, lam_q1, lam_k1, lam_q2, lam_k2, g_subln, g_kv_norm,
           w_uv, w_out, g_norm_ffn, w_peer_q, peer_keys, peer_u, peer_v, rel_bias, g_final):
    B, S, D = x.shape
    depth = w_ada.shape[0]
    bias_diff = rel_bias[:, :DIFF_HEADS]
    bias_dsa = rel_bias[:, DIFF_HEADS:]
    for l in range(depth):
        mod3 = _adaln(c, w_ada[l], b_ada[l]).reshape(B, 1, 6 * D)
        lambda_init = 0.8 - 0.6 * math.exp(-0.3 * l)
        lam = (jnp.exp(jnp.sum(lam_q1[l] * lam_k1[l])) - jnp.exp(jnp.sum(lam_q2[l] * lam_k2[l]))
               + lambda_init).astype(F32)

        dk, kvn, ikn, dqT, dvT, sqT, kvT, iqT, iwT = _inproj(x, mod3, g_norm_mix[l], w_in[l], g_kv_norm[l])
        odT = _diff_attention(dk, dqT, dvT, lam, bias_diff, g_subln[l], lambda_init)
        osT = _dsa_attention(ikn, iqT, iwT, kvn, kvT, sqT, w_uv[l], bias_dsa)
        x1, h2 = _outproj(x, odT, osT, w_out[l], mod3, g_norm_ffn[l])

        T = B * S
        h2f = h2.reshape(T, D)
        idx, g = _route(h2f, w_peer_q[l], peer_keys[l])
        w = _expert_act(idx, h2f.reshape(T, 8, 128), g, _pack_rows(peer_u[l]))
        peer_out = _expert_out(idx, w, _pack_rows(peer_v[l])).reshape(B, S, D)
        x = _final(x1, peer_out, mod3, g_final, last_layer=(l == depth - 1))
    return x
```

```python
import functools
import math

import jax
import jax.numpy as jnp
import numpy as np
from jax import lax
from jax.experimental import pallas as pl
from jax.experimental.pallas import tpu as pltpu

F32 = jnp.float32
BF16 = jnp.bfloat16
I32 = jnp.int32

DIFF_HEADS = 4
DIFF_QK = 64
DIFF_V = 128
DSA_HEADS = 4
DSA_LATENT = 128
DSA_V = 128
IDX_HEADS = 8
IDX_DIM = 64
TOPK_MAX = 256
REL_BUCKETS = 32
REL_MAX_DIST = 128
PEER_HEADS = 8
PEER_NKEYS = 128
PEER_HALF = 64
PEER_TOPK = 16
RMS_EPS = 1e-6

NEG = -1e30
INT_MIN = -(2 ** 31)
VMEM_LIMIT = 56 * 1024 * 1024

ATT_TILE = 256
PEER_ROUTE_TILE = 256
PEER_ROWS = PEER_HEADS * PEER_TOPK


def _cparams(sem):
    return pltpu.CompilerParams(dimension_semantics=sem, vmem_limit_bytes=VMEM_LIMIT)


def _dot(a, b):
    return jnp.dot(a, b, preferred_element_type=F32)


def _dot_nt(a, b):
    return lax.dot_general(a, b, (((1,), (1,)), ((), ())), preferred_element_type=F32)


def _dot_tn(a, b):
    return lax.dot_general(a, b, (((0,), (0,)), ((), ())), preferred_element_type=F32)


def _adaln_kernel(c_ref, w_ref, b_ref, o_ref):
    c = c_ref[...]
    ca = c * (1.0 / (1.0 + jnp.exp(-c)))
    o_ref[...] = _dot(ca, w_ref[...]) + b_ref[...]


def _adaln(c, w, b):
    B, D = c.shape
    N = w.shape[1]
    tn = 1024
    return pl.pallas_call(
        _adaln_kernel,
        grid=(N // tn,),
        in_specs=[pl.BlockSpec((B, D), lambda j: (0, 0)),
                  pl.BlockSpec((D, tn), lambda j: (0, j)),
                  pl.BlockSpec((1, tn), lambda j: (0, j))],
        out_specs=pl.BlockSpec((B, tn), lambda j: (0, j)),
        out_shape=jax.ShapeDtypeStruct((B, N), F32),
        compiler_params=_cparams(("arbitrary",)),
        name="adaln",
    )(c, w, b.reshape(1, N))


def _inproj_kernel(x_ref, sc_ref, sh_ref, g_ref,
                   wn_dk, wn_kv, wn_ik, wt_dq, wt_dv, wt_sq, wt_kv, wt_iq, wt_iw,
                   gkv_row, gkv_col,
                   dk_o, kvn_o, ik_o, dqT_o, dvT_o, sqT_o, kvT_o, iqT_o, iwT_o):
    x = x_ref[0]
    ms = jnp.mean(x * x, axis=-1, keepdims=True)
    h = (x * lax.rsqrt(ms + RMS_EPS)) * g_ref[...]
    h = h * (1.0 + sc_ref[0]) + sh_ref[0]
    hb = h.astype(BF16)

    dk_o[0] = _dot(hb, wn_dk[...]).astype(BF16)
    kv = _dot(hb, wn_kv[...])
    kv = kv * lax.rsqrt(jnp.mean(kv * kv, axis=-1, keepdims=True) + RMS_EPS) * gkv_row[...]
    kvn_o[0] = kv.astype(BF16)
    ik_o[0] = _dot(hb, wn_ik[...]).astype(BF16)

    dqT_o[0, 0] = (_dot_nt(wt_dq[...], hb) * (DIFF_QK ** -0.5)).astype(BF16)
    dvT_o[0, 0] = _dot_nt(wt_dv[...], hb).astype(BF16)
    sqT_o[0, 0] = (_dot_nt(wt_sq[...], hb) * (DSA_LATENT ** -0.5)).astype(BF16)
    kvT = _dot_nt(wt_kv[...], hb)
    kvT = kvT * lax.rsqrt(jnp.mean(kvT * kvT, axis=0, keepdims=True) + RMS_EPS) * gkv_col[...]
    kvT_o[0, 0] = kvT.astype(BF16)
    iqT_o[0, 0] = (_dot_nt(wt_iq[...], hb) * (IDX_DIM ** -0.5)).astype(BF16)
    iwT_o[0, 0] = _dot_nt(wt_iw[...], hb) * (IDX_HEADS ** -0.5)


def _inproj(x, mod3, g_norm, w_in, g_kv):
    B, S, D = x.shape
    tm = ATT_TILE
    nt = S // tm
    sizes = (512, 512, 512, 512, 128, 512, 64, 8)
    offs = np.cumsum((0,) + sizes)
    wb = w_in.astype(BF16)
    piece = lambda i: wb[:, offs[i]:offs[i + 1]]
    w_dq, w_dk, w_dv, w_sq, w_kv, w_iq, w_ik, w_iw = [piece(i) for i in range(8)]
    w_iwT = jnp.zeros((16, D), BF16).at[:8].set(w_iw.T)

    full = lambda a: pl.BlockSpec(a.shape, lambda b, i: (0,) * a.ndim)
    nat = lambda f: pl.BlockSpec((1, tm, f), lambda b, i: (b, i, 0))
    tr = lambda f: pl.BlockSpec((1, 1, f, tm), lambda b, i: (b, i, 0, 0))
    ins = [x, mod3, mod3, g_norm.reshape(1, D),
           w_dk, w_kv, w_ik, w_dq.T, w_dv.T, w_sq.T, w_kv.T, w_iq.T, w_iwT,
           g_kv.reshape(1, -1), g_kv.reshape(-1, 1)]
    in_specs = [pl.BlockSpec((1, tm, D), lambda b, i: (b, i, 0)),
                pl.BlockSpec((1, 1, D), lambda b, i: (b, 0, 1)),
                pl.BlockSpec((1, 1, D), lambda b, i: (b, 0, 0)),
                ] + [full(a) for a in ins[3:]]
    out_shape = [jax.ShapeDtypeStruct((B, S, 512), BF16),
                 jax.ShapeDtypeStruct((B, S, 128), BF16),
                 jax.ShapeDtypeStruct((B, S, 64), BF16),
                 jax.ShapeDtypeStruct((B, nt, 512, tm), BF16),
                 jax.ShapeDtypeStruct((B, nt, 512, tm), BF16),
                 jax.ShapeDtypeStruct((B, nt, 512, tm), BF16),
                 jax.ShapeDtypeStruct((B, nt, 128, tm), BF16),
                 jax.ShapeDtypeStruct((B, nt, 512, tm), BF16),
                 jax.ShapeDtypeStruct((B, nt, 16, tm), F32)]
    out_specs = [nat(512), nat(128), nat(64), tr(512), tr(512), tr(512), tr(128), tr(512), tr(16)]
    return pl.pallas_call(
        _inproj_kernel, grid=(B, nt), in_specs=in_specs, out_specs=out_specs, out_shape=out_shape,
        compiler_params=_cparams(("parallel", "parallel")), name="inproj",
    )(*ins)


def _attend_blocks(keys_of, values_of, q, bias_of, nblk):
    t = q.shape[-1]
    logits = []
    for ki in range(nblk):
        lg = _dot(keys_of(ki), q)
        for term in bias_of(ki):
            lg = lg + term
        logits.append(lg)
    m8 = logits[0].reshape(t // 8, 8, t).max(axis=0)
    for lg in logits[1:]:
        m8 = jnp.maximum(m8, lg.reshape(t // 8, 8, t).max(axis=0))
    m = jnp.max(m8, axis=0, keepdims=True)
    l8 = jnp.zeros((8, t), F32)
    acc = None
    for ki in range(nblk):
        p = jnp.exp(logits[ki] - m)
        l8 = l8 + p.reshape(t // 8, 8, t).sum(axis=0)
        pv = _dot(values_of(ki), p.astype(BF16))
        acc = pv if acc is None else acc + pv
    return acc / jnp.sum(l8, axis=0, keepdims=True)


def _per_query_tile(qi, nt, branch):
    for n in range(nt):
        pl.when(qi == n)(functools.partial(branch, n + 1))


def _rel_bias_tiles(rel_bias_h, t):
    def bucket(dist):
        n = jnp.maximum(dist, 0)
        max_exact = REL_BUCKETS // 2
        nf = jnp.maximum(n, max_exact).astype(F32)
        large = max_exact + (jnp.log(nf / max_exact) / math.log(REL_MAX_DIST / max_exact)
                             * (REL_BUCKETS - max_exact)).astype(I32)
        large = jnp.minimum(large, REL_BUCKETS - 1)
        return jnp.where(n < max_exact, n, large)
    s = jnp.arange(t, dtype=I32)[:, None]
    q = jnp.arange(t, dtype=I32)[None, :]
    d_diag = q - s
    d_sub = q - s + t
    tb = rel_bias_h.astype(F32).T

    def lookup(bk):
        out = jnp.zeros((tb.shape[0],) + bk.shape, F32)
        for k in range(REL_BUCKETS):
            out = jnp.where(bk[None] == k, tb[:, k][:, None, None], out)
        return out
    diag = jnp.where(d_diag >= 0, lookup(bucket(d_diag)), NEG)
    sub = lookup(bucket(d_sub))
    far = tb[:, REL_BUCKETS - 1]
    assert t >= REL_MAX_DIST
    return diag, sub, far


def _diff_kernel(lam_ref, bfar_ref, kn_ref, qT_ref, vT_ref, bd_ref, bs_ref, gsub_ref, o_ref,
                 *, lambda_init, nt):
    h = pl.program_id(1)
    qi = pl.program_id(2)
    t = qT_ref.shape[-1]
    qT = qT_ref[0, 0]
    row = lax.broadcasted_iota(I32, qT.shape, 0)
    zero = jnp.zeros_like(qT)
    q1 = jnp.where(row < DIFF_QK, qT, zero)
    q2 = jnp.where(row >= DIFF_QK, qT, zero)
    bfar = bfar_ref[h]
    lam = lam_ref[0]

    def branch(nblk):
        keys_of = lambda ki: kn_ref[0, ki * t:(ki + 1) * t, :]
        values_of = lambda ki: vT_ref[0, ki]

        def bias_of(ki):
            return (bd_ref[0] if ki == nblk - 1 else bs_ref[0] if ki == nblk - 2 else bfar,)
        o1 = _attend_blocks(keys_of, values_of, q1, bias_of, nblk)
        o2 = _attend_blocks(keys_of, values_of, q2, bias_of, nblk)
        o = o1 - lam * o2
        o = o * lax.rsqrt(jnp.mean(o * o, axis=0, keepdims=True) + RMS_EPS) * gsub_ref[...]
        o_ref[0, 0] = (o * (1.0 - lambda_init)).astype(BF16)

    _per_query_tile(qi, nt, branch)


def _diff_attention(dk, dqT, dvT, lam, bias_diff, g_subln, lambda_init):
    B, S, _ = dk.shape
    nt, t = dqT.shape[1], dqT.shape[3]
    H = DIFF_HEADS
    bd, bs, bfar = _rel_bias_tiles(bias_diff, t)
    smem = pl.BlockSpec(memory_space=pltpu.SMEM)
    return pl.pallas_call(
        functools.partial(_diff_kernel, lambda_init=lambda_init, nt=nt),
        grid=(B, H, nt),
        in_specs=[smem, smem,
                  pl.BlockSpec((1, S, 128), lambda b, h, i: (b, 0, h)),
                  pl.BlockSpec((1, 1, 128, t), lambda b, h, i: (b, i, h, 0)),
                  pl.BlockSpec((1, nt, 128, t), lambda b, h, i: (b, 0, h, 0)),
                  pl.BlockSpec((1, t, t), lambda b, h, i: (h, 0, 0)),
                  pl.BlockSpec((1, t, t), lambda b, h, i: (h, 0, 0)),
                  pl.BlockSpec((DIFF_V, 1), lambda b, h, i: (0, 0))],
        out_specs=pl.BlockSpec((1, 1, 128, t), lambda b, h, i: (b, i, h, 0)),
        out_shape=jax.ShapeDtypeStruct((B, nt, H * DIFF_V, t), BF16),
        compiler_params=_cparams(("parallel", "parallel", "arbitrary")),
        name="diff_attention",
    )(lam.reshape(1), bfar, dk, dqT, dvT, bd, bs, g_subln.reshape(-1, 1).astype(F32))


def _dsa_kernel(bfar_ref, ikn_ref, iqT_ref, iwT_ref, kvn_ref, kvT_ref, sqT_ref, bd_ref, bs_ref,
                wuvT_ref, o_ref, key_s, madd_s, *, k_top, nt):
    qi = pl.program_id(1)
    t = iqT_ref.shape[-1]

    iqT = iqT_ref[0, 0]
    iwT = iwT_ref[0, 0]
    srow = lax.broadcasted_iota(I32, (t, t), 0)
    tcol = lax.broadcasted_iota(I32, (t, t), 1)

    def score_body(ki, c):
        ik = ikn_ref[0, pl.ds(pl.multiple_of(ki * t, t), t), :]
        sc = jnp.zeros((t, t), F32)
        for hh in range(IDX_HEADS):
            d = _dot(ik, iqT[hh * IDX_DIM:(hh + 1) * IDX_DIM, :])
            sc = sc + jnp.maximum(d, 0.0) * iwT[hh:hh + 1, :]
        sc = jnp.where((ki < qi) | (srow <= tcol), sc, -jnp.inf)
        bits = lax.bitcast_convert_type(sc, I32)
        key_s[ki] = bits ^ ((bits >> 31) & jnp.int32(0x7FFFFFFF))
        return c
    lax.fori_loop(0, qi + 1, score_body, 0)

    def branch(nblk):
        def count(pred):
            acc = jnp.zeros((8, t), I32)
            for ki in range(nblk):
                acc = acc + jnp.sum(pred(key_s[ki], ki).astype(I32).reshape(t // 8, 8, t), axis=0)
            return jnp.sum(acc, axis=0, keepdims=True)

        zero_t = jnp.zeros((1, t), I32)
        thr = jnp.where(count(lambda k, ki: k >= zero_t) >= k_top, zero_t,
                        jnp.full((1, t), INT_MIN, I32))

        def bit_body(i, thr):
            cand = thr + (jnp.int32(1) << (30 - i))
            return jnp.where(count(lambda k, ki: k >= cand) >= k_top, cand, thr)
        thr = lax.fori_loop(0, 31, bit_body, thr)

        need = k_top - count(lambda k, ki: k > thr)
        n_eq = count(lambda k, ki: k == thr)
        has_ties = jnp.max(jnp.where(n_eq > need, 1, 0)) > 0
        n_pos = nblk * t

        def tie_search():
            nbits = max(1, int(math.ceil(math.log2(n_pos))))

            def body(i, lo):
                cand = lo + (jnp.int32(1) << (nbits - 1 - i))
                few = count(lambda k, ki: (k == thr) & (srow + ki * t <= cand - 1)) < need
                return jnp.where(few, cand, lo)
            return lax.fori_loop(0, nbits, body, jnp.zeros((1, t), I32))
        jmax = lax.cond(has_ties, tie_search, lambda: jnp.full((1, t), n_pos, I32))

        for ki in range(nblk):
            k = key_s[ki]
            sel = (k > thr) | ((k == thr) & (srow + ki * t <= jmax))
            madd_s[ki] = jnp.where(sel, 0.0, NEG).astype(F32)

        keys_of = lambda ki: kvn_ref[0, ki * t:(ki + 1) * t, :]
        values_of = lambda ki: kvT_ref[0, ki]

        def head(hh, c):
            rows = pl.ds(pl.multiple_of(hh * DSA_LATENT, DSA_LATENT), DSA_LATENT)
            bfar = bfar_ref[hh]

            def bias_of(ki):
                tile = bd_ref[hh] if ki == nblk - 1 else bs_ref[hh] if ki == nblk - 2 else bfar
                return (tile, madd_s[ki])
            oT = _attend_blocks(keys_of, values_of, sqT_ref[0, 0, rows, :], bias_of, nblk)
            o_ref[0, 0, rows, :] = _dot(wuvT_ref[hh], oT.astype(BF16)).astype(BF16)
            return c
        lax.fori_loop(0, DSA_HEADS, head, 0)

    _per_query_tile(qi, nt, branch)


def _dsa_attention(ikn, iqT, iwT, kvn, kvT, sqT, w_uv, bias_dsa):
    B, S, _ = kvn.shape
    nt, t = iqT.shape[1], iqT.shape[3]
    k_top = min(TOPK_MAX, S // 4)
    bd, bs, bfar = _rel_bias_tiles(bias_dsa, t)
    wuvT = jnp.swapaxes(w_uv, 1, 2).astype(BF16)
    smem = pl.BlockSpec(memory_space=pltpu.SMEM)
    full = lambda a: pl.BlockSpec(a.shape, lambda b, i: (0,) * a.ndim)
    return pl.pallas_call(
        functools.partial(_dsa_kernel, k_top=k_top, nt=nt),
        grid=(B, nt),
        in_specs=[smem,
                  pl.BlockSpec((1, S, IDX_DIM), lambda b, i: (b, 0, 0)),
                  pl.BlockSpec((1, 1, 512, t), lambda b, i: (b, i, 0, 0)),
                  pl.BlockSpec((1, 1, 16, t), lambda b, i: (b, i, 0, 0)),
                  pl.BlockSpec((1, S, DSA_LATENT), lambda b, i: (b, 0, 0)),
                  pl.BlockSpec((1, nt, DSA_LATENT, t), lambda b, i: (b, 0, 0, 0)),
                  pl.BlockSpec((1, 1, 512, t), lambda b, i: (b, i, 0, 0)),
                  full(bd), full(bs), full(wuvT)],
        out_specs=pl.BlockSpec((1, 1, 512, t), lambda b, i: (b, i, 0, 0)),
        out_shape=jax.ShapeDtypeStruct((B, nt, DSA_HEADS * DSA_V, t), BF16),
        scratch_shapes=[pltpu.VMEM((nt, t, t), I32), pltpu.VMEM((nt, t, t), F32)],
        compiler_params=_cparams(("parallel", "arbitrary")),
        name="dsa_attention",
    )(bfar, ikn, iqT, iwT, kvn, kvT, sqT, bd, bs, wuvT)


def _outproj_kernel(x_ref, odT_ref, osT_ref, wa_ref, wb_ref, ga_ref, g2_ref, sc_ref, sh_ref,
                    x1_ref, h2_ref):
    mix = _dot_tn(odT_ref[0, 0], wa_ref[...]) + _dot_tn(osT_ref[0, 0], wb_ref[...])
    x1 = x_ref[0] + ga_ref[0] * mix
    x1_ref[0] = x1
    ms = jnp.mean(x1 * x1, axis=-1, keepdims=True)
    h = (x1 * lax.rsqrt(ms + RMS_EPS)) * g2_ref[...]
    h2_ref[0] = h * (1.0 + sc_ref[0]) + sh_ref[0]


def _outproj(x, odT, osT, w_out, mod3, g_norm_ffn):
    B, S, D = x.shape
    nt, t = odT.shape[1], odT.shape[3]
    wb = w_out.astype(BF16)
    wa, wbb = wb[:512], wb[512:]
    full = lambda a: pl.BlockSpec(a.shape, lambda b, i: (0,) * a.ndim)
    modspec = lambda j: pl.BlockSpec((1, 1, D), lambda b, i: (b, 0, j))
    xspec = pl.BlockSpec((1, t, D), lambda b, i: (b, i, 0))
    tspec = pl.BlockSpec((1, 1, 512, t), lambda b, i: (b, i, 0, 0))
    return pl.pallas_call(
        _outproj_kernel, grid=(B, nt),
        in_specs=[xspec, tspec, tspec, full(wa), full(wbb), modspec(2),
                  pl.BlockSpec((1, D), lambda b, i: (0, 0)), modspec(4), modspec(3)],
        out_specs=[xspec, xspec],
        out_shape=[jax.ShapeDtypeStruct((B, S, D), F32), jax.ShapeDtypeStruct((B, S, D), F32)],
        compiler_params=_cparams(("parallel", "parallel")), name="outproj",
    )(x, odT, osT, wa, wbb, mod3, g_norm_ffn.reshape(1, D), mod3, mod3)


def _extract_top(x, pos, payload, n):
    big = jnp.int32(2 ** 30)
    vals, pays = [], []
    for _ in range(n):
        m = jnp.max(x, axis=0, keepdims=True)
        first = jnp.min(jnp.where(x == m, pos, big), axis=0, keepdims=True)
        hit = pos == first
        if payload is None:
            pays.append(first)
        else:
            pays.append(jnp.max(jnp.where(hit, payload, -1), axis=0, keepdims=True))
        vals.append(m)
        x = jnp.where(hit, -jnp.inf, x)
    return jnp.concatenate(vals, axis=0), jnp.concatenate(pays, axis=0)


_CAND_PAIRS = [(a, b) for a in range(PEER_TOPK) for b in range(PEER_TOPK // (a + 1))]


def _route_kernel(h2_ref, wqT_ref, keys_ref, idx_ref, g_ref, idx_s, g_s):
    tm = h2_ref.shape[0]
    hb = h2_ref[...].astype(BF16)
    qT = _dot_nt(wqT_ref[...], hb).astype(BF16)
    krow = lax.broadcasted_iota(I32, (PEER_NKEYS, tm), 0)
    npair = len(_CAND_PAIRS)
    npad = (-npair) % 8
    cpos = jnp.concatenate(
        [jnp.full((1, tm), a * PEER_TOPK + b, I32) for a, b in _CAND_PAIRS]
        + [jnp.full((npad, tm), 2 ** 20, I32)], axis=0)
    for hh in range(PEER_HEADS):
        tops = []
        for p in range(2):
            r0 = (hh * 2 + p) * PEER_HALF
            sc = _dot(keys_ref[hh, p], qT[r0:r0 + PEER_HALF, :])
            tops.append(_extract_top(sc, krow, None, PEER_TOPK))
        (s1, i1), (s2, i2) = tops
        cand = jnp.concatenate(
            [s1[a:a + 1] + s2[b:b + 1] for a, b in _CAND_PAIRS]
            + [jnp.full((npad, tm), -jnp.inf, F32)], axis=0)
        cidx = jnp.concatenate(
            [i1[a:a + 1] * PEER_NKEYS + i2[b:b + 1] for a, b in _CAND_PAIRS]
            + [jnp.zeros((npad, tm), I32)], axis=0)
        top, e_idx = _extract_top(cand, cpos, cidx, PEER_TOPK)
        ex = jnp.exp(top - top[0:1])
        g = ex / jnp.sum(ex, axis=0, keepdims=True)
        idx_s[hh * PEER_TOPK:(hh + 1) * PEER_TOPK, :] = e_idx
        g_s[hh * PEER_TOPK:(hh + 1) * PEER_TOPK, :] = g
    idx_ref[...] = (idx_s[...] * PACK_ROWS).T
    g_ref[...] = g_s[...].T


def _route(h2, w_q, keys):
    T, D = h2.shape
    tm = min(PEER_ROUTE_TILE, T)
    wqT = w_q.T.astype(BF16)
    kb = keys.astype(BF16)
    return pl.pallas_call(
        _route_kernel, grid=(T // tm,),
        in_specs=[pl.BlockSpec((tm, D), lambda i: (i, 0)),
                  pl.BlockSpec(wqT.shape, lambda i: (0, 0)),
                  pl.BlockSpec(kb.shape, lambda i: (0, 0, 0, 0))],
        out_specs=[pl.BlockSpec((tm, PEER_ROWS), lambda i: (i, 0)),
                   pl.BlockSpec((tm, PEER_ROWS), lambda i: (i, 0))],
        out_shape=[jax.ShapeDtypeStruct((T, PEER_ROWS), I32),
                   jax.ShapeDtypeStruct((T, PEER_ROWS), F32)],
        scratch_shapes=[pltpu.VMEM((PEER_ROWS, tm), I32), pltpu.VMEM((PEER_ROWS, tm), F32)],
        compiler_params=_cparams(("parallel",)), name="peer_route",
    )(h2, wqT, kb)


PACK_ROWS = 4


def _pack_rows(tab):
    n, d = tab.shape
    assert d == 2 * PACK_ROWS * 128
    u = lax.bitcast_convert_type(tab.astype(BF16), jnp.uint16).astype(jnp.uint32)
    u = u.reshape(n, 2, PACK_ROWS, 128)
    w = u[:, 0] | (u[:, 1] << 16)
    return lax.bitcast_convert_type(w, I32).reshape(n * PACK_ROWS, 128)


def _load_table_once(tab_hbm, tab_vmem, sem):
    @pl.when(pl.program_id(0) == 0)
    def _():
        cp = pltpu.make_async_copy(tab_hbm, tab_vmem, sem)
        cp.start()
        cp.wait()


EXPERT_TOK_TILE = 128
EXPERT_GROUP = 4


def _expert_masks():
    r = np.arange(8 * PEER_ROWS) % 8
    dmask = (np.arange(8)[:, None] == (4 * (r % 2) + r // 2)[None, :]).astype(np.float32)
    ssum = (np.arange(8 * PEER_ROWS)[:, None] // 8 == np.arange(PEER_ROWS)[None, :]).astype(np.float32)
    return jnp.asarray(dmask), jnp.asarray(ssum, BF16)


def _gather_compute_pipeline(idx_ref, tab, bufs, ngroups, compute):
    def gather(g, buf):
        for u in range(EXPERT_GROUP):
            t = g * EXPERT_GROUP + u
            for k in range(PEER_ROWS):
                off = pl.multiple_of(idx_ref[t, k], PACK_ROWS)
                buf[u, PACK_ROWS * k:PACK_ROWS * (k + 1), :] = tab[pl.ds(off, PACK_ROWS), :]

    gather(0, bufs[0])

    def body(j, c):
        g0 = 2 * j
        gather(g0 + 1, bufs[1])
        compute(g0, bufs[0])
        gather(jnp.minimum(g0 + 2, ngroups - 1), bufs[0])
        compute(g0 + 1, bufs[1])
        return c
    lax.fori_loop(0, ngroups // 2, body, 0)


def _expert_act_kernel(idx_ref, h2_ref, g_ref, dmask_ref, ssum_ref, tab_hbm, w_ref,
                       tab, sem, buf0, buf1, zs):
    _load_table_once(tab_hbm, tab, sem)
    tn = h2_ref.shape[0]
    dmask = dmask_ref[...]

    def compute(g, buf):
        for u in range(EXPERT_GROUP):
            t = g * EXPERT_GROUP + u
            rows = pltpu.bitcast(buf[u], BF16)
            y = _dot_nt(h2_ref[t].astype(BF16), rows)
            zs[pl.ds(pl.multiple_of(t * 8, 8), 8), :] = y * dmask

    _gather_compute_pipeline(idx_ref, tab, (buf0, buf1), tn // EXPERT_GROUP, compute)

    z = zs[...]
    z_hi = z.astype(BF16)
    z_lo = (z - z_hi.astype(F32)).astype(BF16)
    s = _dot(z_hi, ssum_ref[...]) + _dot(z_lo, ssum_ref[...])
    act = jnp.sum(s.reshape(tn, 8, PEER_ROWS), axis=1)
    gelu = 0.5 * act * (1.0 + lax.erf(act * (2.0 ** -0.5)))
    w_ref[...] = g_ref[...] * gelu


def _expert_out_kernel(idx_ref, w_ref, dmask_ref, ssum_ref, tab_hbm, o_ref,
                       tab, sem, buf0, buf1, wrep):
    _load_table_once(tab_hbm, tab, sem)
    tn = o_ref.shape[0]
    dmask = dmask_ref[...]
    wrep[...] = _dot_nt(w_ref[...].astype(BF16), ssum_ref[...])

    def compute(g, buf):
        for u in range(EXPERT_GROUP):
            t = g * EXPERT_GROUP + u
            rows = pltpu.bitcast(buf[u], BF16)
            a = (jnp.broadcast_to(wrep[pl.ds(t, 1), :], dmask.shape) * dmask).astype(BF16)
            o_ref[t] = _dot(a, rows)

    _gather_compute_pipeline(idx_ref, tab, (buf0, buf1), tn // EXPERT_GROUP, compute)


def _expert_pass(body, idx, operands, operand_specs, tab_pk, out_shape, out_spec, extra_scratch, name):
    T = idx.shape[0]
    tn = min(EXPERT_TOK_TILE, T)
    assert T % tn == 0 and (tn // EXPERT_GROUP) % 2 == 0
    dmask, ssum = _expert_masks()
    buf = pltpu.VMEM((EXPERT_GROUP, PACK_ROWS * PEER_ROWS, 128), I32)
    return pl.pallas_call(
        body, grid=(T // tn,),
        in_specs=[pl.BlockSpec((tn, PEER_ROWS), lambda i: (i, 0), memory_space=pltpu.SMEM)]
        + operand_specs(tn)
        + [pl.BlockSpec(dmask.shape, lambda i: (0, 0)), pl.BlockSpec(ssum.shape, lambda i: (0, 0)),
           pl.BlockSpec(memory_space=pl.ANY)],
        out_specs=out_spec(tn), out_shape=out_shape,
        scratch_shapes=[pltpu.VMEM(tab_pk.shape, I32), pltpu.SemaphoreType.DMA, buf, buf]
        + extra_scratch(tn),
        compiler_params=_cparams(("arbitrary",)), name=name,
    )(idx, *operands, dmask, ssum, tab_pk)


def _expert_act(idx, h2r, g, u_pk):
    T = idx.shape[0]
    row = lambda tn: pl.BlockSpec((tn, PEER_ROWS), lambda i: (i, 0))
    return _expert_pass(
        _expert_act_kernel, idx, (h2r, g),
        lambda tn: [pl.BlockSpec((tn, 8, 128), lambda i: (i, 0, 0)), row(tn)], u_pk,
        jax.ShapeDtypeStruct((T, PEER_ROWS), F32), row,
        lambda tn: [pltpu.VMEM((tn * 8, 8 * PEER_ROWS), F32)], "peer_expert_act")


def _expert_out(idx, w, v_pk):
    T = idx.shape[0]
    return _expert_pass(
        _expert_out_kernel, idx, (w,),
        lambda tn: [pl.BlockSpec((tn, PEER_ROWS), lambda i: (i, 0))], v_pk,
        jax.ShapeDtypeStruct((T, 8, 128), F32),
        lambda tn: pl.BlockSpec((tn, 8, 128), lambda i: (i, 0, 0)),
        lambda tn: [pltpu.VMEM((tn, 8 * PEER_ROWS), F32)], "peer_expert_out")


def _final_kernel(x1_ref, p_ref, gf_ref, g_ref, o_ref, *, last_layer):
    x = x1_ref[0] + gf_ref[0] * p_ref[0]
    if last_layer:
        ms = jnp.mean(x * x, axis=-1, keepdims=True)
        x = (x * lax.rsqrt(ms + RMS_EPS)) * g_ref[...]
    o_ref[0] = x


def _final(x1, peer_out, mod3, g_final, last_layer):
    B, S, D = x1.shape
    tm = min(512, S)
    xspec = pl.BlockSpec((1, tm, D), lambda b, i: (b, i, 0))
    return pl.pallas_call(
        functools.partial(_final_kernel, last_layer=last_layer), grid=(B, S // tm),
        in_specs=[xspec, xspec, pl.BlockSpec((1, 1, D), lambda b, i: (b, 0, 5)),
                  pl.BlockSpec((1, D), lambda b, i: (0, 0))],
        out_specs=xspec,
        out_shape=jax.ShapeDtypeStruct((B, S, D), F32),
        compiler_params=_cparams(("parallel", "parallel")), name="final_norm",
    )(x1, peer_out, mod3, g_final.reshape(1, D))


def kernel(x, c, w_ada, b_ada, g_norm_mix, w_in, lam_q1, lam_k1, lam_q2, lam_k2, g_subln, g_kv_norm,
           w_uv, w_out, g_norm_ffn, w_peer_q, peer_keys, peer_u, peer_v, rel_bias, g_final):
    B, S, D = x.shape
    depth = w_ada.shape[0]
    bias_diff = rel_bias[:, :DIFF_HEADS]
    bias_dsa = rel_bias[:, DIFF_HEADS:]
    for l in range(depth):
        mod3 = _adaln(c, w_ada[l], b_ada[l]).reshape(B, 1, 6 * D)
        lambda_init = 0.8 - 0.6 * math.exp(-0.3 * l)
        lam = (jnp.exp(jnp.sum(lam_q1[l] * lam_k1[l])) - jnp.exp(jnp.sum(lam_q2[l] * lam_k2[l]))
               + lambda_init).astype(F32)

        dk, kvn, ikn, dqT, dvT, sqT, kvT, iqT, iwT = _inproj(x, mod3, g_norm_mix[l], w_in[l], g_kv_norm[l])
        odT = _diff_attention(dk, dqT, dvT, lam, bias_diff, g_subln[l], lambda_init)
        osT = _dsa_attention(ikn, iqT, iwT, kvn, kvT, sqT, w_uv[l], bias_dsa)
        x1, h2 = _outproj(x, odT, osT, w_out[l], mod3, g_norm_ffn[l])

        T = B * S
        h2f = h2.reshape(T, D)
        idx, g = _route(h2f, w_peer_q[l], peer_keys[l])
        w = _expert_act(idx, h2f.reshape(T, 8, 128), g, _pack_rows(peer_u[l]))
        peer_out = _expert_out(idx, w, _pack_rows(peer_v[l])).reshape(B, S, D)
        x = _final(x1, peer_out, mod3, g_final, last_layer=(l == depth - 1))
    return x
```

```python
import functools
import math

import jax
import jax.numpy as jnp
import numpy as np
from jax import lax
from jax.experimental import pallas as pl
from jax.experimental.pallas import tpu as pltpu

F32 = jnp.float32
BF16 = jnp.bfloat16
I32 = jnp.int32

DIFF_HEADS = 4
DIFF_QK = 64
DIFF_V = 128
DSA_HEADS = 4
DSA_LATENT = 128
DSA_V = 128
IDX_HEADS = 8
IDX_DIM = 64
TOPK_MAX = 256
REL_BUCKETS = 32
REL_MAX_DIST = 128
PEER_HEADS = 8
PEER_NKEYS = 128
PEER_HALF = 64
PEER_TOPK = 16
RMS_EPS = 1e-6

NEG = -1e30
INT_MIN = -(2 ** 31)
VMEM_LIMIT = 56 * 1024 * 1024

ATT_TILE = 256
PEER_ROUTE_TILE = 256
PEER_ROWS = PEER_HEADS * PEER_TOPK


def _cparams(sem):
    return pltpu.CompilerParams(dimension_semantics=sem, vmem_limit_bytes=VMEM_LIMIT)


def _dot(a, b):
    return jnp.dot(a, b, preferred_element_type=F32)


def _dot_nt(a, b):
    return lax.dot_general(a, b, (((1,), (1,)), ((), ())), preferred_element_type=F32)


def _dot_tn(a, b):
    return lax.dot_general(a, b, (((0,), (0,)), ((), ())), preferred_element_type=F32)


def _adaln_kernel(c_ref, w_ref, b_ref, o_ref):
    c = c_ref[...]
    ca = c * (1.0 / (1.0 + jnp.exp(-c)))
    o_ref[...] = _dot(ca, w_ref[...]) + b_ref[...]


def _adaln(c, w, b):
    B, D = c.shape
    N = w.shape[1]
    tn = 1024
    return pl.pallas_call(
        _adaln_kernel,
        grid=(N // tn,),
        in_specs=[pl.BlockSpec((B, D), lambda j: (0, 0)),
                  pl.BlockSpec((D, tn), lambda j: (0, j)),
                  pl.BlockSpec((1, tn), lambda j: (0, j))],
        out_specs=pl.BlockSpec((B, tn), lambda j: (0, j)),
        out_shape=jax.ShapeDtypeStruct((B, N), F32),
        compiler_params=_cparams(("arbitrary",)),
        name="adaln",
    )(c, w, b.reshape(1, N))


def _inproj_kernel(x_ref, sc_ref, sh_ref, g_ref,
                   wn_dk, wn_kv, wn_ik, wt_dq, wt_dv, wt_sq, wt_kv, wt_iq, wt_iw,
                   gkv_row, gkv_col,
                   dk_o, kvn_o, ik_o, dqT_o, dvT_o, sqT_o, kvT_o, iqT_o, iwT_o):
    x = x_ref[0]
    ms = jnp.mean(x * x, axis=-1, keepdims=True)
    h = (x * lax.rsqrt(ms + RMS_EPS)) * g_ref[...]
    h = h * (1.0 + sc_ref[0]) + sh_ref[0]
    hb = h.astype(BF16)

    dk_o[0] = _dot(hb, wn_dk[...]).astype(BF16)
    kv = _dot(hb, wn_kv[...])
    kv = kv * lax.rsqrt(jnp.mean(kv * kv, axis=-1, keepdims=True) + RMS_EPS) * gkv_row[...]
    kvn_o[0] = kv.astype(BF16)
    ik_o[0] = _dot(hb, wn_ik[...]).astype(BF16)

    dqT_o[0, 0] = (_dot_nt(wt_dq[...], hb) * (DIFF_QK ** -0.5)).astype(BF16)
    dvT_o[0, 0] = _dot_nt(wt_dv[...], hb).astype(BF16)
    sqT_o[0, 0] = (_dot_nt(wt_sq[...], hb) * (DSA_LATENT ** -0.5)).astype(BF16)
    kvT = _dot_nt(wt_kv[...], hb)
    kvT = kvT * lax.rsqrt(jnp.mean(kvT * kvT, axis=0, keepdims=True) + RMS_EPS) * gkv_col[...]
    kvT_o[0, 0] = kvT.astype(BF16)
    iqT_o[0, 0] = (_dot_nt(wt_iq[...], hb) * (IDX_DIM ** -0.5)).astype(BF16)
    iwT_o[0, 0] = _dot_nt(wt_iw[...], hb) * (IDX_HEADS ** -0.5)


def _inproj(x, mod3, g_norm, w_in, g_kv):
    B, S, D = x.shape
    tm = ATT_TILE
    nt = S // tm
    sizes = (512, 512, 512, 512, 128, 512, 64, 8)
    offs = np.cumsum((0,) + sizes)
    wb = w_in.astype(BF16)
    piece = lambda i: wb[:, offs[i]:offs[i + 1]]
    w_dq, w_dk, w_dv, w_sq, w_kv, w_iq, w_ik, w_iw = [piece(i) for i in range(8)]
    w_iwT = jnp.zeros((16, D), BF16).at[:8].set(w_iw.T)

    full = lambda a: pl.BlockSpec(a.shape, lambda b, i: (0,) * a.ndim)
    nat = lambda f: pl.BlockSpec((1, tm, f), lambda b, i: (b, i, 0))
    tr = lambda f: pl.BlockSpec((1, 1, f, tm), lambda b, i: (b, i, 0, 0))
    ins = [x, mod3, mod3, g_norm.reshape(1, D),
           w_dk, w_kv, w_ik, w_dq.T, w_dv.T, w_sq.T, w_kv.T, w_iq.T, w_iwT,
           g_kv.reshape(1, -1), g_kv.reshape(-1, 1)]
    in_specs = [pl.BlockSpec((1, tm, D), lambda b, i: (b, i, 0)),
                pl.BlockSpec((1, 1, D), lambda b, i: (b, 0, 1)),
                pl.BlockSpec((1, 1, D), lambda b, i: (b, 0, 0)),
                ] + [full(a) for a in ins[3:]]
    out_shape = [jax.ShapeDtypeStruct((B, S, 512), BF16),
                 jax.ShapeDtypeStruct((B, S, 128), BF16),
                 jax.ShapeDtypeStruct((B, S, 64), BF16),
                 jax.ShapeDtypeStruct((B, nt, 512, tm), BF16),
                 jax.ShapeDtypeStruct((B, nt, 512, tm), BF16),
                 jax.ShapeDtypeStruct((B, nt, 512, tm), BF16),
                 jax.ShapeDtypeStruct((B, nt, 128, tm), BF16),
                 jax.ShapeDtypeStruct((B, nt, 512, tm), BF16),
                 jax.ShapeDtypeStruct((B, nt, 16, tm), F32)]
    out_specs = [nat(512), nat(128), nat(64), tr(512), tr(512), tr(512), tr(128), tr(512), tr(16)]
    return pl.pallas_call(
        _inproj_kernel, grid=(B, nt), in_specs=in_specs, out_specs=out_specs, out_shape=out_shape,
        compiler_params=_cparams(("parallel", "parallel")), name="inproj",
    )(*ins)


def _attend_blocks(keys_of, values_of, q, bias_of, nblk):
    t = q.shape[-1]
    logits = []
    for ki in range(nblk):
        lg = _dot(keys_of(ki), q)
        for term in bias_of(ki):
            lg = lg + term
        logits.append(lg)
    m8 = logits[0].reshape(t // 8, 8, t).max(axis=0)
    for lg in logits[1:]:
        m8 = jnp.maximum(m8, lg.reshape(t // 8, 8, t).max(axis=0))
    m = jnp.max(m8, axis=0, keepdims=True)
    l8 = jnp.zeros((8, t), F32)
    acc = None
    for ki in range(nblk):
        p = jnp.exp(logits[ki] - m)
        l8 = l8 + p.reshape(t // 8, 8, t).sum(axis=0)
        pv = _dot(values_of(ki), p.astype(BF16))
        acc = pv if acc is None else acc + pv
    return acc / jnp.sum(l8, axis=0, keepdims=True)


def _per_query_tile(qi, nt, branch):
    for n in range(nt):
        pl.when(qi == n)(functools.partial(branch, n + 1))


def _rel_bias_tiles(rel_bias_h, t):
    def bucket(dist):
        n = jnp.maximum(dist, 0)
        max_exact = REL_BUCKETS // 2
        nf = jnp.maximum(n, max_exact).astype(F32)
        large = max_exact + (jnp.log(nf / max_exact) / math.log(REL_MAX_DIST / max_exact)
                             * (REL_BUCKETS - max_exact)).astype(I32)
        large = jnp.minimum(large, REL_BUCKETS - 1)
        return jnp.where(n < max_exact, n, large)
    s = jnp.arange(t, dtype=I32)[:, None]
    q = jnp.arange(t, dtype=I32)[None, :]
    d_diag = q - s
    d_sub = q - s + t
    tb = rel_bias_h.astype(F32).T

    def lookup(bk):
        out = jnp.zeros((tb.shape[0],) + bk.shape, F32)
        for k in range(REL_BUCKETS):
            out = jnp.where(bk[None] == k, tb[:, k][:, None, None], out)
        return out
    diag = jnp.where(d_diag >= 0, lookup(bucket(d_diag)), NEG)
    sub = lookup(bucket(d_sub))
    far = tb[:, REL_BUCKETS - 1]
    assert t >= REL_MAX_DIST
    return diag, sub, far


def _diff_kernel(lam_ref, bfar_ref, kn_ref, qT_ref, vT_ref, bd_ref, bs_ref, gsub_ref, o_ref,
                 *, lambda_init, nt):
    h = pl.program_id(1)
    qi = pl.program_id(2)
    t = qT_ref.shape[-1]
    qT = qT_ref[0, 0]
    row = lax.broadcasted_iota(I32, qT.shape, 0)
    zero = jnp.zeros_like(qT)
    q1 = jnp.where(row < DIFF_QK, qT, zero)
    q2 = jnp.where(row >= DIFF_QK, qT, zero)
    bfar = bfar_ref[h]
    lam = lam_ref[0]

    def branch(nblk):
        keys_of = lambda ki: kn_ref[0, ki * t:(ki + 1) * t, :]
        values_of = lambda ki: vT_ref[0, ki]

        def bias_of(ki):
            return (bd_ref[0] if ki == nblk - 1 else bs_ref[0] if ki == nblk - 2 else bfar,)
        o1 = _attend_blocks(keys_of, values_of, q1, bias_of, nblk)
        o2 = _attend_blocks(keys_of, values_of, q2, bias_of, nblk)
        o = o1 - lam * o2
        o = o * lax.rsqrt(jnp.mean(o * o, axis=0, keepdims=True) + RMS_EPS) * gsub_ref[...]
        o_ref[0, 0] = (o * (1.0 - lambda_init)).astype(BF16)

    _per_query_tile(qi, nt, branch)


def _diff_attention(dk, dqT, dvT, lam, bias_diff, g_subln, lambda_init):
    B, S, _ = dk.shape
    nt, t = dqT.shape[1], dqT.shape[3]
    H = DIFF_HEADS
    bd, bs, bfar = _rel_bias_tiles(bias_diff, t)
    smem = pl.BlockSpec(memory_space=pltpu.SMEM)
    return pl.pallas_call(
        functools.partial(_diff_kernel, lambda_init=lambda_init, nt=nt),
        grid=(B, H, nt),
        in_specs=[smem, smem,
                  pl.BlockSpec((1, S, 128), lambda b, h, i: (b, 0, h)),
                  pl.BlockSpec((1, 1, 128, t), lambda b, h, i: (b, i, h, 0)),
                  pl.BlockSpec((1, nt, 128, t), lambda b, h, i: (b, 0, h, 0)),
                  pl.BlockSpec((1, t, t), lambda b, h, i: (h, 0, 0)),
                  pl.BlockSpec((1, t, t), lambda b, h, i: (h, 0, 0)),
                  pl.BlockSpec((DIFF_V, 1), lambda b, h, i: (0, 0))],
        out_specs=pl.BlockSpec((1, 1, 128, t), lambda b, h, i: (b, i, h, 0)),
        out_shape=jax.ShapeDtypeStruct((B, nt, H * DIFF_V, t), BF16),
        compiler_params=_cparams(("parallel", "parallel", "arbitrary")),
        name="diff_attention",
    )(lam.reshape(1), bfar, dk, dqT, dvT, bd, bs, g_subln.reshape(-1, 1).astype(F32))


def _dsa_kernel(bfar_ref, ikn_ref, iqT_ref, iwT_ref, kvn_ref, kvT_ref, sqT_ref, bd_ref, bs_ref,
                wuvT_ref, o_ref, key_s, madd_s, *, k_top, nt):
    qi = pl.program_id(1)
    t = iqT_ref.shape[-1]

    iqT = iqT_ref[0, 0]
    iwT = iwT_ref[0, 0]
    srow = lax.broadcasted_iota(I32, (t, t), 0)
    tcol = lax.broadcasted_iota(I32, (t, t), 1)

    def score_body(ki, c):
        ik = ikn_ref[0, pl.ds(pl.multiple_of(ki * t, t), t), :]
        sc = jnp.zeros((t, t), F32)
        for hh in range(IDX_HEADS):
            d = _dot(ik, iqT[hh * IDX_DIM:(hh + 1) * IDX_DIM, :])
            sc = sc + jnp.maximum(d, 0.0) * iwT[hh:hh + 1, :]
        sc = jnp.where((ki < qi) | (srow <= tcol), sc, -jnp.inf)
        bits = lax.bitcast_convert_type(sc, I32)
        key_s[ki] = bits ^ ((bits >> 31) & jnp.int32(0x7FFFFFFF))
        return c
    lax.fori_loop(0, qi + 1, score_body, 0)

    def branch(nblk):
        def count(pred):
            acc = jnp.zeros((8, t), I32)
            for ki in range(nblk):
                acc = acc + jnp.sum(pred(key_s[ki], ki).astype(I32).reshape(t // 8, 8, t), axis=0)
            return jnp.sum(acc, axis=0, keepdims=True)

        zero_t = jnp.zeros((1, t), I32)
        thr = jnp.where(count(lambda k, ki: k >= zero_t) >= k_top, zero_t,
                        jnp.full((1, t), INT_MIN, I32))

        def bit_body(i, thr):
            cand = thr + (jnp.int32(1) << (30 - i))
            return jnp.where(count(lambda k, ki: k >= cand) >= k_top, cand, thr)
        thr = lax.fori_loop(0, 31, bit_body, thr)

        need = k_top - count(lambda k, ki: k > thr)
        n_eq = count(lambda k, ki: k == thr)
        has_ties = jnp.max(jnp.where(n_eq > need, 1, 0)) > 0
        n_pos = nblk * t

        def tie_search():
            nbits = max(1, int(math.ceil(math.log2(n_pos))))

            def body(i, lo):
                cand = lo + (jnp.int32(1) << (nbits - 1 - i))
                few = count(lambda k, ki: (k == thr) & (srow + ki * t <= cand - 1)) < need
                return jnp.where(few, cand, lo)
            return lax.fori_loop(0, nbits, body, jnp.zeros((1, t), I32))
        jmax = lax.cond(has_ties, tie_search, lambda: jnp.full((1, t), n_pos, I32))

        for ki in range(nblk):
            k = key_s[ki]
            sel = (k > thr) | ((k == thr) & (srow + ki * t <= jmax))
            madd_s[ki] = jnp.where(sel, 0.0, NEG).astype(F32)

        keys_of = lambda ki: kvn_ref[0, ki * t:(ki + 1) * t, :]
        values_of = lambda ki: kvT_ref[0, ki]

        def head(hh, c):
            rows = pl.ds(pl.multiple_of(hh * DSA_LATENT, DSA_LATENT), DSA_LATENT)
            bfar = bfar_ref[hh]

            def bias_of(ki):
                tile = bd_ref[hh] if ki == nblk - 1 else bs_ref[hh] if ki == nblk - 2 else bfar
                return (tile, madd_s[ki])
            oT = _attend_blocks(keys_of, values_of, sqT_ref[0, 0, rows, :], bias_of, nblk)
            o_ref[0, 0, rows, :] = _dot(wuvT_ref[hh], oT.astype(BF16)).astype(BF16)
            return c
        lax.fori_loop(0, DSA_HEADS, head, 0)

    _per_query_tile(qi, nt, branch)


def _dsa_attention(ikn, iqT, iwT, kvn, kvT, sqT, w_uv, bias_dsa):
    B, S, _ = kvn.shape
    nt, t = iqT.shape[1], iqT.shape[3]
    k_top = min(TOPK_MAX, S // 4)
    bd, bs, bfar = _rel_bias_tiles(bias_dsa, t)
    wuvT = jnp.swapaxes(w_uv, 1, 2).astype(BF16)
    smem = pl.BlockSpec(memory_space=pltpu.SMEM)
    full = lambda a: pl.BlockSpec(a.shape, lambda b, i: (0,) * a.ndim)
    return pl.pallas_call(
        functools.partial(_dsa_kernel, k_top=k_top, nt=nt),
        grid=(B, nt),
        in_specs=[smem,
                  pl.BlockSpec((1, S, IDX_DIM), lambda b, i: (b, 0, 0)),
                  pl.BlockSpec((1, 1, 512, t), lambda b, i: (b, i, 0, 0)),
                  pl.BlockSpec((1, 1, 16, t), lambda b, i: (b, i, 0, 0)),
                  pl.BlockSpec((1, S, DSA_LATENT), lambda b, i: (b, 0, 0)),
                  pl.BlockSpec((1, nt, DSA_LATENT, t), lambda b, i: (b, 0, 0, 0)),
                  pl.BlockSpec((1, 1, 512, t), lambda b, i: (b, i, 0, 0)),
                  full(bd), full(bs), full(wuvT)],
        out_specs=pl.BlockSpec((1, 1, 512, t), lambda b, i: (b, i, 0, 0)),
        out_shape=jax.ShapeDtypeStruct((B, nt, DSA_HEADS * DSA_V, t), BF16),
        scratch_shapes=[pltpu.VMEM((nt, t, t), I32), pltpu.VMEM((nt, t, t), F32)],
        compiler_params=_cparams(("parallel", "arbitrary")),
        name="dsa_attention",
    )(bfar, ikn, iqT, iwT, kvn, kvT, sqT, bd, bs, wuvT)


def _outproj_kernel(x_ref, odT_ref, osT_ref, wa_ref, wb_ref, ga_ref, g2_ref, sc_ref, sh_ref,
                    x1_ref, h2_ref):
    mix = _dot_tn(odT_ref[0, 0], wa_ref[...]) + _dot_tn(osT_ref[0, 0], wb_ref[...])
    x1 = x_ref[0] + ga_ref[0] * mix
    x1_ref[0] = x1
    ms = jnp.mean(x1 * x1, axis=-1, keepdims=True)
    h = (x1 * lax.rsqrt(ms + RMS_EPS)) * g2_ref[...]
    h2_ref[0] = h * (1.0 + sc_ref[0]) + sh_ref[0]


def _outproj(x, odT, osT, w_out, mod3, g_norm_ffn):
    B, S, D = x.shape
    nt, t = odT.shape[1], odT.shape[3]
    wb = w_out.astype(BF16)
    wa, wbb = wb[:512], wb[512:]
    full = lambda a: pl.BlockSpec(a.shape, lambda b, i: (0,) * a.ndim)
    modspec = lambda j: pl.BlockSpec((1, 1, D), lambda b, i: (b, 0, j))
    xspec = pl.BlockSpec((1, t, D), lambda b, i: (b, i, 0))
    tspec = pl.BlockSpec((1, 1, 512, t), lambda b, i: (b, i, 0, 0))
    return pl.pallas_call(
        _outproj_kernel, grid=(B, nt),
        in_specs=[xspec, tspec, tspec, full(wa), full(wbb), modspec(2),
                  pl.BlockSpec((1, D), lambda b, i: (0, 0)), modspec(4), modspec(3)],
        out_specs=[xspec, xspec],
        out_shape=[jax.ShapeDtypeStruct((B, S, D), F32), jax.ShapeDtypeStruct((B, S, D), F32)],
        compiler_params=_cparams(("parallel", "parallel")), name="outproj",
    )(x, odT, osT, wa, wbb, mod3, g_norm_ffn.reshape(1, D), mod3, mod3)


def _extract_top(x, pos, payload, n):
    big = jnp.int32(2 ** 30)
    vals, pays = [], []
    for _ in range(n):
        m = jnp.max(x, axis=0, keepdims=True)
        first = jnp.min(jnp.where(x == m, pos, big), axis=0, keepdims=True)
        hit = pos == first
        if payload is None:
            pays.append(first)
        else:
            pays.append(jnp.max(jnp.where(hit, payload, -1), axis=0, keepdims=True))
        vals.append(m)
        x = jnp.where(hit, -jnp.inf, x)
    return jnp.concatenate(vals, axis=0), jnp.concatenate(pays, axis=0)


_CAND_PAIRS = [(a, b) for a in range(PEER_TOPK) for b in range(PEER_TOPK // (a + 1))]


def _route_kernel(h2_ref, wqT_ref, keys_ref, idx_ref, g_ref, idx_s, g_s):
    tm = h2_ref.shape[0]
    hb = h2_ref[...].astype(BF16)
    qT = _dot_nt(wqT_ref[...], hb).astype(BF16)
    krow = lax.broadcasted_iota(I32, (PEER_NKEYS, tm), 0)
    npair = len(_CAND_PAIRS)
    npad = (-npair) % 8
    cpos = jnp.concatenate(
        [jnp.full((1, tm), a * PEER_TOPK + b, I32) for a, b in _CAND_PAIRS]
        + [jnp.full((npad, tm), 2 ** 20, I32)], axis=0)
    for hh in range(PEER_HEADS):
        tops = []
        for p in range(2):
            r0 = (hh * 2 + p) * PEER_HALF
            sc = _dot(keys_ref[hh, p], qT[r0:r0 + PEER_HALF, :])
            tops.append(_extract_top(sc, krow, None, PEER_TOPK))
        (s1, i1), (s2, i2) = tops
        cand = jnp.concatenate(
            [s1[a:a + 1] + s2[b:b + 1] for a, b in _CAND_PAIRS]
            + [jnp.full((npad, tm), -jnp.inf, F32)], axis=0)
        cidx = jnp.concatenate(
            [i1[a:a + 1] * PEER_NKEYS + i2[b:b + 1] for a, b in _CAND_PAIRS]
            + [jnp.zeros((npad, tm), I32)], axis=0)
        top, e_idx = _extract_top(cand, cpos, cidx, PEER_TOPK)
        ex = jnp.exp(top - top[0:1])
        g = ex / jnp.sum(ex, axis=0, keepdims=True)
        idx_s[hh * PEER_TOPK:(hh + 1) * PEER_TOPK, :] = e_idx
        g_s[hh * PEER_TOPK:(hh + 1) * PEER_TOPK, :] = g
    idx_ref[...] = (idx_s[...] * PACK_ROWS).T
    g_ref[...] = g_s[...].T


def _route(h2, w_q, keys):
    T, D = h2.shape
    tm = min(PEER_ROUTE_TILE, T)
    wqT = w_q.T.astype(BF16)
    kb = keys.astype(BF16)
    return pl.pallas_call(
        _route_kernel, grid=(T // tm,),
        in_specs=[pl.BlockSpec((tm, D), lambda i: (i, 0)),
                  pl.BlockSpec(wqT.shape, lambda i: (0, 0)),
                  pl.BlockSpec(kb.shape, lambda i: (0, 0, 0, 0))],
        out_specs=[pl.BlockSpec((tm, PEER_ROWS), lambda i: (i, 0)),
                   pl.BlockSpec((tm, PEER_ROWS), lambda i: (i, 0))],
        out_shape=[jax.ShapeDtypeStruct((T, PEER_ROWS), I32),
                   jax.ShapeDtypeStruct((T, PEER_ROWS), F32)],
        scratch_shapes=[pltpu.VMEM((PEER_ROWS, tm), I32), pltpu.VMEM((PEER_ROWS, tm), F32)],
        compiler_params=_cparams(("parallel",)), name="peer_route",
    )(h2, wqT, kb)


PACK_ROWS = 4


def _pack_rows(tab):
    n, d = tab.shape
    assert d == 2 * PACK_ROWS * 128
    u = lax.bitcast_convert_type(tab.astype(BF16), jnp.uint16).astype(jnp.uint32)
    u = u.reshape(n, 2, PACK_ROWS, 128)
    w = u[:, 0] | (u[:, 1] << 16)
    return lax.bitcast_convert_type(w, I32).reshape(n * PACK_ROWS, 128)


def _load_table_once(tab_hbm, tab_vmem, sem):
    @pl.when(pl.program_id(0) == 0)
    def _():
        cp = pltpu.make_async_copy(tab_hbm, tab_vmem, sem)
        cp.start()
        cp.wait()


EXPERT_TOK_TILE = 128
EXPERT_GROUP = 8


def _expert_masks():
    r = np.arange(8 * PEER_ROWS) % 8
    dmask = (np.arange(8)[:, None] == (4 * (r % 2) + r // 2)[None, :]).astype(np.float32)
    ssum = (np.arange(8 * PEER_ROWS)[:, None] // 8 == np.arange(PEER_ROWS)[None, :]).astype(np.float32)
    return jnp.asarray(dmask), jnp.asarray(ssum, BF16)


IDX_BUFS = 4


def _gather_compute_pipeline(idx_hbm, tab, sbufs, sems, vbufs, ngroups, compute):
    step = pl.program_id(0)
    nsteps = pl.num_programs(0)
    assert ngroups % IDX_BUFS == 0
    last = nsteps * ngroups - 1

    def idx_copy(gg, b):
        row0 = pl.multiple_of(jnp.minimum(gg, last) * EXPERT_GROUP, EXPERT_GROUP)
        return pltpu.make_async_copy(idx_hbm.at[pl.ds(row0, EXPERT_GROUP), :], sbufs[b], sems.at[b])

    def gather(b, vbuf):
        for u in range(EXPERT_GROUP):
            for k in range(PEER_ROWS):
                off = pl.multiple_of(sbufs[b][u, k], PACK_ROWS)
                vbuf[u, PACK_ROWS * k:PACK_ROWS * (k + 1), :] = tab[pl.ds(off, PACK_ROWS), :]

    @pl.when(step == 0)
    def _():
        for b in range(IDX_BUFS - 1):
            idx_copy(b, b).start()
        idx_copy(0, 0).wait()
        gather(0, vbufs[0])

    def body(jb, c):
        for s in range(IDX_BUFS):
            g = jb * IDX_BUFS + s
            gg = step * ngroups + g
            idx_copy(gg + 3, (s + 3) % IDX_BUFS).start()
            idx_copy(gg + 1, (s + 1) % IDX_BUFS).wait()
            gather((s + 1) % IDX_BUFS, vbufs[(s + 1) % 2])
            compute(g, vbufs[s % 2])
        return c
    lax.fori_loop(0, ngroups // IDX_BUFS, body, 0)

    @pl.when(step == nsteps - 1)
    def _():
        idx_copy(last, 1).wait()
        idx_copy(last, 2).wait()


def _expert_act_kernel(idx_hbm, h2_ref, g_ref, dmask_ref, ssum_ref, tab_hbm, w_ref,
                       tab, sem, buf0, buf1, sb0, sb1, sb2, sb3, isems, zs):
    _load_table_once(tab_hbm, tab, sem)
    tn = h2_ref.shape[0]
    dmask = dmask_ref[...]

    def compute(g, buf):
        for u in range(EXPERT_GROUP):
            t = g * EXPERT_GROUP + u
            rows = pltpu.bitcast(buf[u], BF16)
            y = _dot_nt(h2_ref[t].astype(BF16), rows)
            zs[pl.ds(pl.multiple_of(t * 8, 8), 8), :] = y * dmask

    _gather_compute_pipeline(idx_hbm, tab, (sb0, sb1, sb2, sb3), isems, (buf0, buf1),
                             tn // EXPERT_GROUP, compute)

    z = zs[...]
    z_hi = z.astype(BF16)
    z_lo = (z - z_hi.astype(F32)).astype(BF16)
    s = _dot(z_hi, ssum_ref[...]) + _dot(z_lo, ssum_ref[...])
    act = jnp.sum(s.reshape(tn, 8, PEER_ROWS), axis=1)
    gelu = 0.5 * act * (1.0 + lax.erf(act * (2.0 ** -0.5)))
    w_ref[...] = g_ref[...] * gelu


def _expert_out_kernel(idx_hbm, w_ref, dmask_ref, ssum_ref, tab_hbm, o_ref,
                       tab, sem, buf0, buf1, sb0, sb1, sb2, sb3, isems, wrep):
    _load_table_once(tab_hbm, tab, sem)
    tn = o_ref.shape[0]
    dmask = dmask_ref[...]
    wrep[...] = _dot_nt(w_ref[...].astype(BF16), ssum_ref[...])

    def compute(g, buf):
        for u in range(EXPERT_GROUP):
            t = g * EXPERT_GROUP + u
            rows = pltpu.bitcast(buf[u], BF16)
            a = (jnp.broadcast_to(wrep[pl.ds(t, 1), :], dmask.shape) * dmask).astype(BF16)
            o_ref[t] = _dot(a, rows)

    _gather_compute_pipeline(idx_hbm, tab, (sb0, sb1, sb2, sb3), isems, (buf0, buf1),
                             tn // EXPERT_GROUP, compute)


def _expert_pass(body, idx, operands, operand_specs, tab_pk, out_shape, out_spec, extra_scratch, name):
    T = idx.shape[0]
    tn = min(EXPERT_TOK_TILE, T)
    assert T % tn == 0
    dmask, ssum = _expert_masks()
    buf = pltpu.VMEM((EXPERT_GROUP, PACK_ROWS * PEER_ROWS, 128), I32)
    sbuf = pltpu.SMEM((EXPERT_GROUP, PEER_ROWS), I32)
    return pl.pallas_call(
        body, grid=(T // tn,),
        in_specs=[pl.BlockSpec(memory_space=pl.ANY)]
        + operand_specs(tn)
        + [pl.BlockSpec(dmask.shape, lambda i: (0, 0)), pl.BlockSpec(ssum.shape, lambda i: (0, 0)),
           pl.BlockSpec(memory_space=pl.ANY)],
        out_specs=out_spec(tn), out_shape=out_shape,
        scratch_shapes=[pltpu.VMEM(tab_pk.shape, I32), pltpu.SemaphoreType.DMA, buf, buf]
        + [sbuf] * IDX_BUFS + [pltpu.SemaphoreType.DMA((IDX_BUFS,))] + extra_scratch(tn),
        compiler_params=_cparams(("arbitrary",)), name=name,
    )(idx, *operands, dmask, ssum, tab_pk)


def _expert_act(idx, h2r, g, u_pk):
    T = idx.shape[0]
    row = lambda tn: pl.BlockSpec((tn, PEER_ROWS), lambda i: (i, 0))
    return _expert_pass(
        _expert_act_kernel, idx, (h2r, g),
        lambda tn: [pl.BlockSpec((tn, 8, 128), lambda i: (i, 0, 0)), row(tn)], u_pk,
        jax.ShapeDtypeStruct((T, PEER_ROWS), F32), row,
        lambda tn: [pltpu.VMEM((tn * 8, 8 * PEER_ROWS), F32)], "peer_expert_act")


def _expert_out(idx, w, v_pk):
    T = idx.shape[0]
    return _expert_pass(
        _expert_out_kernel, idx, (w,),
        lambda tn: [pl.BlockSpec((tn, PEER_ROWS), lambda i: (i, 0))], v_pk,
        jax.ShapeDtypeStruct((T, 8, 128), F32),
        lambda tn: pl.BlockSpec((tn, 8, 128), lambda i: (i, 0, 0)),
        lambda tn: [pltpu.VMEM((tn, 8 * PEER_ROWS), F32)], "peer_expert_out")


def _final_kernel(x1_ref, p_ref, gf_ref, g_ref, o_ref, *, last_layer):
    x = x1_ref[0] + gf_ref[0] * p_ref[0]
    if last_layer:
        ms = jnp.mean(x * x, axis=-1, keepdims=True)
        x = (x * lax.rsqrt(ms + RMS_EPS)) * g_ref[...]
    o_ref[0] = x


def _final(x1, peer_out, mod3, g_final, last_layer):
    B, S, D = x1.shape
    tm = min(512, S)
    xspec = pl.BlockSpec((1, tm, D), lambda b, i: (b, i, 0))
    return pl.pallas_call(
        functools.partial(_final_kernel, last_layer=last_layer), grid=(B, S // tm),
        in_specs=[xspec, xspec, pl.BlockSpec((1, 1, D), lambda b, i: (b, 0, 5)),
                  pl.BlockSpec((1, D), lambda b, i: (0, 0))],
        out_specs=xspec,
        out_shape=jax.ShapeDtypeStruct((B, S, D), F32),
        compiler_params=_cparams(("parallel", "parallel")), name="final_norm",
    )(x1, peer_out, mod3, g_final.reshape(1, D))


def kernel(x, c, w_ada, b_ada, g_norm_mix, w_in, lam_q1, lam_k1, lam_q2, lam_k2, g_subln, g_kv_norm,
           w_uv, w_out, g_norm_ffn, w_peer_q, peer_keys, peer_u, peer_v, rel_bias, g_final):
    B, S, D = x.shape
    depth = w_ada.shape[0]
    bias_diff = rel_bias[:, :DIFF_HEADS]
    bias_dsa = rel_bias[:, DIFF_HEADS:]
    for l in range(depth):
        mod3 = _adaln(c, w_ada[l], b_ada[l]).reshape(B, 1, 6 * D)
        lambda_init = 0.8 - 0.6 * math.exp(-0.3 * l)
        lam = (jnp.exp(jnp.sum(lam_q1[l] * lam_k1[l])) - jnp.exp(jnp.sum(lam_q2[l] * lam_k2[l]))
               + lambda_init).astype(F32)

        dk, kvn, ikn, dqT, dvT, sqT, kvT, iqT, iwT = _inproj(x, mod3, g_norm_mix[l], w_in[l], g_kv_norm[l])
        odT = _diff_attention(dk, dqT, dvT, lam, bias_diff, g_subln[l], lambda_init)
        osT = _dsa_attention(ikn, iqT, iwT, kvn, kvT, sqT, w_uv[l], bias_dsa)
        x1, h2 = _outproj(x, odT, osT, w_out[l], mod3, g_norm_ffn[l])

        T = B * S
        h2f = h2.reshape(T, D)
        idx, g = _route(h2f, w_peer_q[l], peer_keys[l])
        w = _expert_act(idx, h2f.reshape(T, 8, 128), g, _pack_rows(peer_u[l]))
        peer_out = _expert_out(idx, w, _pack_rows(peer_v[l])).reshape(B, S, D)
        x = _final(x1, peer_out, mod3, g_final, last_layer=(l == depth - 1))
    return x
```

```python
import functools
import math

import jax
import jax.numpy as jnp
import numpy as np
from jax import lax
from jax.experimental import pallas as pl
from jax.experimental.pallas import tpu as pltpu

F32 = jnp.float32
BF16 = jnp.bfloat16
I32 = jnp.int32

DIFF_HEADS = 4
DIFF_QK = 64
DIFF_V = 128
DSA_HEADS = 4
DSA_LATENT = 128
DSA_V = 128
IDX_HEADS = 8
IDX_DIM = 64
TOPK_MAX = 256
REL_BUCKETS = 32
REL_MAX_DIST = 128
PEER_HEADS = 8
PEER_NKEYS = 128
PEER_HALF = 64
PEER_TOPK = 16
RMS_EPS = 1e-6

NEG = -1e30
INT_MIN = -(2 ** 31)
VMEM_LIMIT = 56 * 1024 * 1024

ATT_TILE = 256
PEER_ROUTE_TILE = 256
PEER_ROWS = PEER_HEADS * PEER_TOPK


def _cparams(sem):
    return pltpu.CompilerParams(dimension_semantics=sem, vmem_limit_bytes=VMEM_LIMIT)


def _dot(a, b):
    return jnp.dot(a, b, preferred_element_type=F32)


def _dot_nt(a, b):
    return lax.dot_general(a, b, (((1,), (1,)), ((), ())), preferred_element_type=F32)


def _dot_tn(a, b):
    return lax.dot_general(a, b, (((0,), (0,)), ((), ())), preferred_element_type=F32)


def _adaln_kernel(c_ref, w_ref, b_ref, o_ref):
    c = c_ref[...]
    ca = c * (1.0 / (1.0 + jnp.exp(-c)))
    o_ref[...] = _dot(ca, w_ref[...]) + b_ref[...]


def _adaln(c, w, b):
    B, D = c.shape
    N = w.shape[1]
    tn = 1024
    return pl.pallas_call(
        _adaln_kernel,
        grid=(N // tn,),
        in_specs=[pl.BlockSpec((B, D), lambda j: (0, 0)),
                  pl.BlockSpec((D, tn), lambda j: (0, j)),
                  pl.BlockSpec((1, tn), lambda j: (0, j))],
        out_specs=pl.BlockSpec((B, tn), lambda j: (0, j)),
        out_shape=jax.ShapeDtypeStruct((B, N), F32),
        compiler_params=_cparams(("arbitrary",)),
        name="adaln",
    )(c, w, b.reshape(1, N))


def _inproj_kernel(x_ref, sc_ref, sh_ref, g_ref,
                   wn_dk, wn_kv, wn_ik, wt_dq, wt_dv, wt_sq, wt_kv, wt_iq, wt_iw,
                   gkv_row, gkv_col,
                   dk_o, kvn_o, ik_o, dqT_o, dvT_o, sqT_o, kvT_o, iqT_o, iwT_o):
    x = x_ref[0]
    ms = jnp.mean(x * x, axis=-1, keepdims=True)
    h = (x * lax.rsqrt(ms + RMS_EPS)) * g_ref[...]
    h = h * (1.0 + sc_ref[0]) + sh_ref[0]
    hb = h.astype(BF16)

    dk_o[0] = _dot(hb, wn_dk[...]).astype(BF16)
    kv = _dot(hb, wn_kv[...])
    kv = kv * lax.rsqrt(jnp.mean(kv * kv, axis=-1, keepdims=True) + RMS_EPS) * gkv_row[...]
    kvn_o[0] = kv.astype(BF16)
    ik_o[0] = _dot(hb, wn_ik[...]).astype(BF16)

    dqT_o[0, 0] = (_dot_nt(wt_dq[...], hb) * (DIFF_QK ** -0.5)).astype(BF16)
    dvT_o[0, 0] = _dot_nt(wt_dv[...], hb).astype(BF16)
    sqT_o[0, 0] = (_dot_nt(wt_sq[...], hb) * (DSA_LATENT ** -0.5)).astype(BF16)
    kvT = _dot_nt(wt_kv[...], hb)
    kvT = kvT * lax.rsqrt(jnp.mean(kvT * kvT, axis=0, keepdims=True) + RMS_EPS) * gkv_col[...]
    kvT_o[0, 0] = kvT.astype(BF16)
    iqT_o[0, 0] = (_dot_nt(wt_iq[...], hb) * (IDX_DIM ** -0.5)).astype(BF16)
    iwT_o[0, 0] = _dot_nt(wt_iw[...], hb) * (IDX_HEADS ** -0.5)


def _inproj(x, mod3, g_norm, w_in, g_kv):
    B, S, D = x.shape
    tm = ATT_TILE
    nt = S // tm
    sizes = (512, 512, 512, 512, 128, 512, 64, 8)
    offs = np.cumsum((0,) + sizes)
    wb = w_in.astype(BF16)
    piece = lambda i: wb[:, offs[i]:offs[i + 1]]
    w_dq, w_dk, w_dv, w_sq, w_kv, w_iq, w_ik, w_iw = [piece(i) for i in range(8)]
    w_iwT = jnp.zeros((16, D), BF16).at[:8].set(w_iw.T)

    full = lambda a: pl.BlockSpec(a.shape, lambda b, i: (0,) * a.ndim)
    nat = lambda f: pl.BlockSpec((1, tm, f), lambda b, i: (b, i, 0))
    tr = lambda f: pl.BlockSpec((1, 1, f, tm), lambda b, i: (b, i, 0, 0))
    ins = [x, mod3, mod3, g_norm.reshape(1, D),
           w_dk, w_kv, w_ik, w_dq.T, w_dv.T, w_sq.T, w_kv.T, w_iq.T, w_iwT,
           g_kv.reshape(1, -1), g_kv.reshape(-1, 1)]
    in_specs = [pl.BlockSpec((1, tm, D), lambda b, i: (b, i, 0)),
                pl.BlockSpec((1, 1, D), lambda b, i: (b, 0, 1)),
                pl.BlockSpec((1, 1, D), lambda b, i: (b, 0, 0)),
                ] + [full(a) for a in ins[3:]]
    out_shape = [jax.ShapeDtypeStruct((B, S, 512), BF16),
                 jax.ShapeDtypeStruct((B, S, 128), BF16),
                 jax.ShapeDtypeStruct((B, S, 64), BF16),
                 jax.ShapeDtypeStruct((B, nt, 512, tm), BF16),
                 jax.ShapeDtypeStruct((B, nt, 512, tm), BF16),
                 jax.ShapeDtypeStruct((B, nt, 512, tm), BF16),
                 jax.ShapeDtypeStruct((B, nt, 128, tm), BF16),
                 jax.ShapeDtypeStruct((B, nt, 512, tm), BF16),
                 jax.ShapeDtypeStruct((B, nt, 16, tm), F32)]
    out_specs = [nat(512), nat(128), nat(64), tr(512), tr(512), tr(512), tr(128), tr(512), tr(16)]
    return pl.pallas_call(
        _inproj_kernel, grid=(B, nt), in_specs=in_specs, out_specs=out_specs, out_shape=out_shape,
        compiler_params=_cparams(("parallel", "parallel")), name="inproj",
    )(*ins)


def _attend_blocks(keys_of, values_of, q, bias_of, nblk):
    t = q.shape[-1]
    logits = []
    for ki in range(nblk):
        lg = _dot(keys_of(ki), q)
        for term in bias_of(ki):
            lg = lg + term
        logits.append(lg)
    m8 = logits[0].reshape(t // 8, 8, t).max(axis=0)
    for lg in logits[1:]:
        m8 = jnp.maximum(m8, lg.reshape(t // 8, 8, t).max(axis=0))
    m = jnp.max(m8, axis=0, keepdims=True)
    l8 = jnp.zeros((8, t), F32)
    acc = None
    for ki in range(nblk):
        p = jnp.exp(logits[ki] - m)
        l8 = l8 + p.reshape(t // 8, 8, t).sum(axis=0)
        pv = _dot(values_of(ki), p.astype(BF16))
        acc = pv if acc is None else acc + pv
    return acc / jnp.sum(l8, axis=0, keepdims=True)


def _per_query_tile(qi, nt, branch):
    for n in range(nt):
        pl.when(qi == n)(functools.partial(branch, n + 1))


def _rel_bias_tiles(rel_bias_h, t):
    def bucket(dist):
        n = jnp.maximum(dist, 0)
        max_exact = REL_BUCKETS // 2
        nf = jnp.maximum(n, max_exact).astype(F32)
        large = max_exact + (jnp.log(nf / max_exact) / math.log(REL_MAX_DIST / max_exact)
                             * (REL_BUCKETS - max_exact)).astype(I32)
        large = jnp.minimum(large, REL_BUCKETS - 1)
        return jnp.where(n < max_exact, n, large)
    s = jnp.arange(t, dtype=I32)[:, None]
    q = jnp.arange(t, dtype=I32)[None, :]
    d_diag = q - s
    d_sub = q - s + t
    tb = rel_bias_h.astype(F32).T

    def lookup(bk):
        out = jnp.zeros((tb.shape[0],) + bk.shape, F32)
        for k in range(REL_BUCKETS):
            out = jnp.where(bk[None] == k, tb[:, k][:, None, None], out)
        return out
    diag = jnp.where(d_diag >= 0, lookup(bucket(d_diag)), NEG)
    sub = lookup(bucket(d_sub))
    far = tb[:, REL_BUCKETS - 1]
    assert t >= REL_MAX_DIST
    return diag, sub, far


def _diff_kernel(lam_ref, bfar_ref, kn_ref, qT_ref, vT_ref, bd_ref, bs_ref, gsub_ref, o_ref,
                 *, lambda_init, nt):
    h = pl.program_id(1)
    qi = pl.program_id(2)
    t = qT_ref.shape[-1]
    qT = qT_ref[0, 0]
    row = lax.broadcasted_iota(I32, qT.shape, 0)
    zero = jnp.zeros_like(qT)
    q1 = jnp.where(row < DIFF_QK, qT, zero)
    q2 = jnp.where(row >= DIFF_QK, qT, zero)
    bfar = bfar_ref[h]
    lam = lam_ref[0]

    def branch(nblk):
        keys_of = lambda ki: kn_ref[0, ki * t:(ki + 1) * t, :]
        values_of = lambda ki: vT_ref[0, ki]

        def bias_of(ki):
            return (bd_ref[0] if ki == nblk - 1 else bs_ref[0] if ki == nblk - 2 else bfar,)
        o1 = _attend_blocks(keys_of, values_of, q1, bias_of, nblk)
        o2 = _attend_blocks(keys_of, values_of, q2, bias_of, nblk)
        o = o1 - lam * o2
        o = o * lax.rsqrt(jnp.mean(o * o, axis=0, keepdims=True) + RMS_EPS) * gsub_ref[...]
        o_ref[0, 0] = (o * (1.0 - lambda_init)).astype(BF16)

    _per_query_tile(qi, nt, branch)


def _diff_attention(dk, dqT, dvT, lam, bias_diff, g_subln, lambda_init):
    B, S, _ = dk.shape
    nt, t = dqT.shape[1], dqT.shape[3]
    H = DIFF_HEADS
    bd, bs, bfar = _rel_bias_tiles(bias_diff, t)
    smem = pl.BlockSpec(memory_space=pltpu.SMEM)
    return pl.pallas_call(
        functools.partial(_diff_kernel, lambda_init=lambda_init, nt=nt),
        grid=(B, H, nt),
        in_specs=[smem, smem,
                  pl.BlockSpec((1, S, 128), lambda b, h, i: (b, 0, h)),
                  pl.BlockSpec((1, 1, 128, t), lambda b, h, i: (b, i, h, 0)),
                  pl.BlockSpec((1, nt, 128, t), lambda b, h, i: (b, 0, h, 0)),
                  pl.BlockSpec((1, t, t), lambda b, h, i: (h, 0, 0)),
                  pl.BlockSpec((1, t, t), lambda b, h, i: (h, 0, 0)),
                  pl.BlockSpec((DIFF_V, 1), lambda b, h, i: (0, 0))],
        out_specs=pl.BlockSpec((1, 1, 128, t), lambda b, h, i: (b, i, h, 0)),
        out_shape=jax.ShapeDtypeStruct((B, nt, H * DIFF_V, t), BF16),
        compiler_params=_cparams(("parallel", "parallel", "arbitrary")),
        name="diff_attention",
    )(lam.reshape(1), bfar, dk, dqT, dvT, bd, bs, g_subln.reshape(-1, 1).astype(F32))


def _dsa_kernel(bfar_ref, ikn_ref, iqT_ref, iwT_ref, kvn_ref, kvT_ref, sqT_ref, bd_ref, bs_ref,
                wuvT_ref, o_ref, key_s, madd_s, *, k_top, nt):
    qi = pl.program_id(1)
    t = iqT_ref.shape[-1]

    iqT = iqT_ref[0, 0]
    iwT = iwT_ref[0, 0]
    srow = lax.broadcasted_iota(I32, (t, t), 0)
    tcol = lax.broadcasted_iota(I32, (t, t), 1)

    def score_body(ki, c):
        ik = ikn_ref[0, pl.ds(pl.multiple_of(ki * t, t), t), :]
        sc = jnp.zeros((t, t), F32)
        for hh in range(IDX_HEADS):
            d = _dot(ik, iqT[hh * IDX_DIM:(hh + 1) * IDX_DIM, :])
            sc = sc + jnp.maximum(d, 0.0) * iwT[hh:hh + 1, :]
        sc = jnp.where((ki < qi) | (srow <= tcol), sc, -jnp.inf)
        bits = lax.bitcast_convert_type(sc, I32)
        key_s[ki] = bits ^ ((bits >> 31) & jnp.int32(0x7FFFFFFF))
        return c
    lax.fori_loop(0, qi + 1, score_body, 0)

    def branch(nblk):
        def count(pred):
            acc = jnp.zeros((8, t), I32)
            for ki in range(nblk):
                acc = acc + jnp.sum(pred(key_s[ki], ki).astype(I32).reshape(t // 8, 8, t), axis=0)
            return jnp.sum(acc, axis=0, keepdims=True)

        zero_t = jnp.zeros((1, t), I32)
        thr = jnp.where(count(lambda k, ki: k >= zero_t) >= k_top, zero_t,
                        jnp.full((1, t), INT_MIN, I32))

        def bit_body(i, thr):
            cand = thr + (jnp.int32(1) << (30 - i))
            return jnp.where(count(lambda k, ki: k >= cand) >= k_top, cand, thr)
        thr = lax.fori_loop(0, 31, bit_body, thr)

        need = k_top - count(lambda k, ki: k > thr)
        n_eq = count(lambda k, ki: k == thr)
        has_ties = jnp.max(jnp.where(n_eq > need, 1, 0)) > 0
        n_pos = nblk * t

        def tie_search():
            nbits = max(1, int(math.ceil(math.log2(n_pos))))

            def body(i, lo):
                cand = lo + (jnp.int32(1) << (nbits - 1 - i))
                few = count(lambda k, ki: (k == thr) & (srow + ki * t <= cand - 1)) < need
                return jnp.where(few, cand, lo)
            return lax.fori_loop(0, nbits, body, jnp.zeros((1, t), I32))
        jmax = lax.cond(has_ties, tie_search, lambda: jnp.full((1, t), n_pos, I32))

        for ki in range(nblk):
            k = key_s[ki]
            sel = (k > thr) | ((k == thr) & (srow + ki * t <= jmax))
            madd_s[ki] = jnp.where(sel, 0.0, NEG).astype(F32)

        keys_of = lambda ki: kvn_ref[0, ki * t:(ki + 1) * t, :]
        values_of = lambda ki: kvT_ref[0, ki]

        def head(hh, c):
            rows = pl.ds(pl.multiple_of(hh * DSA_LATENT, DSA_LATENT), DSA_LATENT)
            bfar = bfar_ref[hh]

            def bias_of(ki):
                tile = bd_ref[hh] if ki == nblk - 1 else bs_ref[hh] if ki == nblk - 2 else bfar
                return (tile, madd_s[ki])
            oT = _attend_blocks(keys_of, values_of, sqT_ref[0, 0, rows, :], bias_of, nblk)
            o_ref[0, 0, rows, :] = _dot(wuvT_ref[hh], oT.astype(BF16)).astype(BF16)
            return c
        lax.fori_loop(0, DSA_HEADS, head, 0)

    _per_query_tile(qi, nt, branch)


def _dsa_attention(ikn, iqT, iwT, kvn, kvT, sqT, w_uv, bias_dsa):
    B, S, _ = kvn.shape
    nt, t = iqT.shape[1], iqT.shape[3]
    k_top = min(TOPK_MAX, S // 4)
    bd, bs, bfar = _rel_bias_tiles(bias_dsa, t)
    wuvT = jnp.swapaxes(w_uv, 1, 2).astype(BF16)
    smem = pl.BlockSpec(memory_space=pltpu.SMEM)
    full = lambda a: pl.BlockSpec(a.shape, lambda b, i: (0,) * a.ndim)
    return pl.pallas_call(
        functools.partial(_dsa_kernel, k_top=k_top, nt=nt),
        grid=(B, nt),
        in_specs=[smem,
                  pl.BlockSpec((1, S, IDX_DIM), lambda b, i: (b, 0, 0)),
                  pl.BlockSpec((1, 1, 512, t), lambda b, i: (b, i, 0, 0)),
                  pl.BlockSpec((1, 1, 16, t), lambda b, i: (b, i, 0, 0)),
                  pl.BlockSpec((1, S, DSA_LATENT), lambda b, i: (b, 0, 0)),
                  pl.BlockSpec((1, nt, DSA_LATENT, t), lambda b, i: (b, 0, 0, 0)),
                  pl.BlockSpec((1, 1, 512, t), lambda b, i: (b, i, 0, 0)),
                  full(bd), full(bs), full(wuvT)],
        out_specs=pl.BlockSpec((1, 1, 512, t), lambda b, i: (b, i, 0, 0)),
        out_shape=jax.ShapeDtypeStruct((B, nt, DSA_HEADS * DSA_V, t), BF16),
        scratch_shapes=[pltpu.VMEM((nt, t, t), I32), pltpu.VMEM((nt, t, t), F32)],
        compiler_params=_cparams(("parallel", "arbitrary")),
        name="dsa_attention",
    )(bfar, ikn, iqT, iwT, kvn, kvT, sqT, bd, bs, wuvT)


def _outproj_kernel(x_ref, odT_ref, osT_ref, wa_ref, wb_ref, ga_ref, g2_ref, sc_ref, sh_ref,
                    x1_ref, h2_ref):
    mix = _dot_tn(odT_ref[0, 0], wa_ref[...]) + _dot_tn(osT_ref[0, 0], wb_ref[...])
    x1 = x_ref[0] + ga_ref[0] * mix
    x1_ref[0] = x1
    ms = jnp.mean(x1 * x1, axis=-1, keepdims=True)
    h = (x1 * lax.rsqrt(ms + RMS_EPS)) * g2_ref[...]
    h2_ref[0] = h * (1.0 + sc_ref[0]) + sh_ref[0]


def _outproj(x, odT, osT, w_out, mod3, g_norm_ffn):
    B, S, D = x.shape
    nt, t = odT.shape[1], odT.shape[3]
    wb = w_out.astype(BF16)
    wa, wbb = wb[:512], wb[512:]
    full = lambda a: pl.BlockSpec(a.shape, lambda b, i: (0,) * a.ndim)
    modspec = lambda j: pl.BlockSpec((1, 1, D), lambda b, i: (b, 0, j))
    xspec = pl.BlockSpec((1, t, D), lambda b, i: (b, i, 0))
    tspec = pl.BlockSpec((1, 1, 512, t), lambda b, i: (b, i, 0, 0))
    return pl.pallas_call(
        _outproj_kernel, grid=(B, nt),
        in_specs=[xspec, tspec, tspec, full(wa), full(wbb), modspec(2),
                  pl.BlockSpec((1, D), lambda b, i: (0, 0)), modspec(4), modspec(3)],
        out_specs=[xspec, xspec],
        out_shape=[jax.ShapeDtypeStruct((B, S, D), F32), jax.ShapeDtypeStruct((B, S, D), F32)],
        compiler_params=_cparams(("parallel", "parallel")), name="outproj",
    )(x, odT, osT, wa, wbb, mod3, g_norm_ffn.reshape(1, D), mod3, mod3)


def _extract_top(x, pos, payload, n):
    big = jnp.int32(2 ** 30)
    vals, pays = [], []
    for _ in range(n):
        m = jnp.max(x, axis=0, keepdims=True)
        first = jnp.min(jnp.where(x == m, pos, big), axis=0, keepdims=True)
        hit = pos == first
        if payload is None:
            pays.append(first)
        else:
            pays.append(jnp.max(jnp.where(hit, payload, -1), axis=0, keepdims=True))
        vals.append(m)
        x = jnp.where(hit, -jnp.inf, x)
    return jnp.concatenate(vals, axis=0), jnp.concatenate(pays, axis=0)


_CAND_PAIRS = [(a, b) for a in range(PEER_TOPK) for b in range(PEER_TOPK // (a + 1))]


def _route_kernel(h2_ref, wqT_ref, keys_ref, idx_ref, g_ref, idx_s, g_s):
    tm = h2_ref.shape[0]
    hb = h2_ref[...].astype(BF16)
    qT = _dot_nt(wqT_ref[...], hb).astype(BF16)
    krow = lax.broadcasted_iota(I32, (PEER_NKEYS, tm), 0)
    npair = len(_CAND_PAIRS)
    npad = (-npair) % 8
    cpos = jnp.concatenate(
        [jnp.full((1, tm), a * PEER_TOPK + b, I32) for a, b in _CAND_PAIRS]
        + [jnp.full((npad, tm), 2 ** 20, I32)], axis=0)
    for hh in range(PEER_HEADS):
        tops = []
        for p in range(2):
            r0 = (hh * 2 + p) * PEER_HALF
            sc = _dot(keys_ref[hh, p], qT[r0:r0 + PEER_HALF, :])
            tops.append(_extract_top(sc, krow, None, PEER_TOPK))
        (s1, i1), (s2, i2) = tops
        cand = jnp.concatenate(
            [s1[a:a + 1] + s2[b:b + 1] for a, b in _CAND_PAIRS]
            + [jnp.full((npad, tm), -jnp.inf, F32)], axis=0)
        cidx = jnp.concatenate(
            [i1[a:a + 1] * PEER_NKEYS + i2[b:b + 1] for a, b in _CAND_PAIRS]
            + [jnp.zeros((npad, tm), I32)], axis=0)
        top, e_idx = _extract_top(cand, cpos, cidx, PEER_TOPK)
        ex = jnp.exp(top - top[0:1])
        g = ex / jnp.sum(ex, axis=0, keepdims=True)
        idx_s[hh * PEER_TOPK:(hh + 1) * PEER_TOPK, :] = e_idx
        g_s[hh * PEER_TOPK:(hh + 1) * PEER_TOPK, :] = g
    idx_ref[...] = (idx_s[...] * PACK_ROWS).T
    g_ref[...] = g_s[...].T


def _route(h2, w_q, keys):
    T, D = h2.shape
    tm = min(PEER_ROUTE_TILE, T)
    wqT = w_q.T.astype(BF16)
    kb = keys.astype(BF16)
    return pl.pallas_call(
        _route_kernel, grid=(T // tm,),
        in_specs=[pl.BlockSpec((tm, D), lambda i: (i, 0)),
                  pl.BlockSpec(wqT.shape, lambda i: (0, 0)),
                  pl.BlockSpec(kb.shape, lambda i: (0, 0, 0, 0))],
        out_specs=[pl.BlockSpec((tm, PEER_ROWS), lambda i: (i, 0)),
                   pl.BlockSpec((tm, PEER_ROWS), lambda i: (i, 0))],
        out_shape=[jax.ShapeDtypeStruct((T, PEER_ROWS), I32),
                   jax.ShapeDtypeStruct((T, PEER_ROWS), F32)],
        scratch_shapes=[pltpu.VMEM((PEER_ROWS, tm), I32), pltpu.VMEM((PEER_ROWS, tm), F32)],
        compiler_params=_cparams(("parallel",)), name="peer_route",
    )(h2, wqT, kb)


PACK_ROWS = 4


def _pack_rows(tab):
    n, d = tab.shape
    assert d == 2 * PACK_ROWS * 128
    u = lax.bitcast_convert_type(tab.astype(BF16), jnp.uint16).astype(jnp.uint32)
    u = u.reshape(n, 2, PACK_ROWS, 128)
    w = u[:, 0] | (u[:, 1] << 16)
    return lax.bitcast_convert_type(w, I32).reshape(n * PACK_ROWS, 128)


def _load_table_once(tab_hbm, tab_vmem, sem):
    @pl.when(pl.program_id(0) == 0)
    def _():
        cp = pltpu.make_async_copy(tab_hbm, tab_vmem, sem)
        cp.start()
        cp.wait()


EXPERT_TOK_TILE = 128
EXPERT_GROUP = 16


def _expert_masks():
    r = np.arange(8 * PEER_ROWS) % 8
    dmask = (np.arange(8)[:, None] == (4 * (r % 2) + r // 2)[None, :]).astype(np.float32)
    ssum = (np.arange(8 * PEER_ROWS)[:, None] // 8 == np.arange(PEER_ROWS)[None, :]).astype(np.float32)
    return jnp.asarray(dmask), jnp.asarray(ssum, BF16)


IDX_BUFS = 4


def _gather_compute_pipeline(idx_hbm, tab, sbufs, sems, vbufs, ngroups, compute):
    step = pl.program_id(0)
    nsteps = pl.num_programs(0)
    assert ngroups % IDX_BUFS == 0
    last = nsteps * ngroups - 1

    def idx_copy(gg, b):
        row0 = pl.multiple_of(jnp.minimum(gg, last) * EXPERT_GROUP, EXPERT_GROUP)
        return pltpu.make_async_copy(idx_hbm.at[pl.ds(row0, EXPERT_GROUP), :], sbufs[b], sems.at[b])

    def gather(b, vbuf):
        for u in range(EXPERT_GROUP):
            for k in range(PEER_ROWS):
                off = pl.multiple_of(sbufs[b][u, k], PACK_ROWS)
                vbuf[u, PACK_ROWS * k:PACK_ROWS * (k + 1), :] = tab[pl.ds(off, PACK_ROWS), :]

    @pl.when(step == 0)
    def _():
        for b in range(IDX_BUFS - 1):
            idx_copy(b, b).start()
        idx_copy(0, 0).wait()
        gather(0, vbufs[0])

    def body(jb, c):
        for s in range(IDX_BUFS):
            g = jb * IDX_BUFS + s
            gg = step * ngroups + g
            idx_copy(gg + 3, (s + 3) % IDX_BUFS).start()
            idx_copy(gg + 1, (s + 1) % IDX_BUFS).wait()
            gather((s + 1) % IDX_BUFS, vbufs[(s + 1) % 2])
            compute(g, vbufs[s % 2])
        return c
    lax.fori_loop(0, ngroups // IDX_BUFS, body, 0)

    @pl.when(step == nsteps - 1)
    def _():
        idx_copy(last, 1).wait()
        idx_copy(last, 2).wait()


def _expert_act_kernel(idx_hbm, h2_ref, g_ref, dmask_ref, ssum_ref, tab_hbm, w_ref,
                       tab, sem, buf0, buf1, sb0, sb1, sb2, sb3, isems, zs):
    _load_table_once(tab_hbm, tab, sem)
    tn = h2_ref.shape[0]
    dmask = dmask_ref[...]

    def compute(g, buf):
        for u in range(EXPERT_GROUP):
            t = g * EXPERT_GROUP + u
            rows = pltpu.bitcast(buf[u], BF16)
            y = _dot_nt(h2_ref[t].astype(BF16), rows)
            zs[pl.ds(pl.multiple_of(t * 8, 8), 8), :] = y * dmask

    _gather_compute_pipeline(idx_hbm, tab, (sb0, sb1, sb2, sb3), isems, (buf0, buf1),
                             tn // EXPERT_GROUP, compute)

    z = zs[...]
    z_hi = z.astype(BF16)
    z_lo = (z - z_hi.astype(F32)).astype(BF16)
    s = _dot(z_hi, ssum_ref[...]) + _dot(z_lo, ssum_ref[...])
    act = jnp.sum(s.reshape(tn, 8, PEER_ROWS), axis=1)
    gelu = 0.5 * act * (1.0 + lax.erf(act * (2.0 ** -0.5)))
    w_ref[...] = g_ref[...] * gelu


def _expert_out_kernel(idx_hbm, w_ref, dmask_ref, ssum_ref, tab_hbm, o_ref,
                       tab, sem, buf0, buf1, sb0, sb1, sb2, sb3, isems, wrep):
    _load_table_once(tab_hbm, tab, sem)
    tn = o_ref.shape[0]
    dmask = dmask_ref[...]
    wrep[...] = _dot_nt(w_ref[...].astype(BF16), ssum_ref[...])

    def compute(g, buf):
        for u in range(EXPERT_GROUP):
            t = g * EXPERT_GROUP + u
            rows = pltpu.bitcast(buf[u], BF16)
            a = (jnp.broadcast_to(wrep[pl.ds(t, 1), :], dmask.shape) * dmask).astype(BF16)
            o_ref[t] = _dot(a, rows)

    _gather_compute_pipeline(idx_hbm, tab, (sb0, sb1, sb2, sb3), isems, (buf0, buf1),
                             tn // EXPERT_GROUP, compute)


def _expert_pass(body, idx, operands, operand_specs, tab_pk, out_shape, out_spec, extra_scratch, name):
    T = idx.shape[0]
    tn = min(EXPERT_TOK_TILE, T)
    assert T % tn == 0
    dmask, ssum = _expert_masks()
    buf = pltpu.VMEM((EXPERT_GROUP, PACK_ROWS * PEER_ROWS, 128), I32)
    sbuf = pltpu.SMEM((EXPERT_GROUP, PEER_ROWS), I32)
    return pl.pallas_call(
        body, grid=(T // tn,),
        in_specs=[pl.BlockSpec(memory_space=pl.ANY)]
        + operand_specs(tn)
        + [pl.BlockSpec(dmask.shape, lambda i: (0, 0)), pl.BlockSpec(ssum.shape, lambda i: (0, 0)),
           pl.BlockSpec(memory_space=pl.ANY)],
        out_specs=out_spec(tn), out_shape=out_shape,
        scratch_shapes=[pltpu.VMEM(tab_pk.shape, I32), pltpu.SemaphoreType.DMA, buf, buf]
        + [sbuf] * IDX_BUFS + [pltpu.SemaphoreType.DMA((IDX_BUFS,))] + extra_scratch(tn),
        compiler_params=_cparams(("arbitrary",)), name=name,
    )(idx, *operands, dmask, ssum, tab_pk)


def _expert_act(idx, h2r, g, u_pk):
    T = idx.shape[0]
    row = lambda tn: pl.BlockSpec((tn, PEER_ROWS), lambda i: (i, 0))
    return _expert_pass(
        _expert_act_kernel, idx, (h2r, g),
        lambda tn: [pl.BlockSpec((tn, 8, 128), lambda i: (i, 0, 0)), row(tn)], u_pk,
        jax.ShapeDtypeStruct((T, PEER_ROWS), F32), row,
        lambda tn: [pltpu.VMEM((tn * 8, 8 * PEER_ROWS), F32)], "peer_expert_act")


def _expert_out(idx, w, v_pk):
    T = idx.shape[0]
    return _expert_pass(
        _expert_out_kernel, idx, (w,),
        lambda tn: [pl.BlockSpec((tn, PEER_ROWS), lambda i: (i, 0))], v_pk,
        jax.ShapeDtypeStruct((T, 8, 128), F32),
        lambda tn: pl.BlockSpec((tn, 8, 128), lambda i: (i, 0, 0)),
        lambda tn: [pltpu.VMEM((tn, 8 * PEER_ROWS), F32)], "peer_expert_out")


def _final_kernel(x1_ref, p_ref, gf_ref, g_ref, o_ref, *, last_layer):
    x = x1_ref[0] + gf_ref[0] * p_ref[0]
    if last_layer:
        ms = jnp.mean(x * x, axis=-1, keepdims=True)
        x = (x * lax.rsqrt(ms + RMS_EPS)) * g_ref[...]
    o_ref[0] = x


def _final(x1, peer_out, mod3, g_final, last_layer):
    B, S, D = x1.shape
    tm = min(512, S)
    xspec = pl.BlockSpec((1, tm, D), lambda b, i: (b, i, 0))
    return pl.pallas_call(
        functools.partial(_final_kernel, last_layer=last_layer), grid=(B, S // tm),
        in_specs=[xspec, xspec, pl.BlockSpec((1, 1, D), lambda b, i: (b, 0, 5)),
                  pl.BlockSpec((1, D), lambda b, i: (0, 0))],
        out_specs=xspec,
        out_shape=jax.ShapeDtypeStruct((B, S, D), F32),
        compiler_params=_cparams(("parallel", "parallel")), name="final_norm",
    )(x1, peer_out, mod3, g_final.reshape(1, D))


def kernel(x, c, w_ada, b_ada, g_norm_mix, w_in, lam_q1, lam_k1, lam_q2, lam_k2, g_subln, g_kv_norm,
           w_uv, w_out, g_norm_ffn, w_peer_q, peer_keys, peer_u, peer_v, rel_bias, g_final):
    B, S, D = x.shape
    depth = w_ada.shape[0]
    bias_diff = rel_bias[:, :DIFF_HEADS]
    bias_dsa = rel_bias[:, DIFF_HEADS:]
    for l in range(depth):
        mod3 = _adaln(c, w_ada[l], b_ada[l]).reshape(B, 1, 6 * D)
        lambda_init = 0.8 - 0.6 * math.exp(-0.3 * l)
        lam = (jnp.exp(jnp.sum(lam_q1[l] * lam_k1[l])) - jnp.exp(jnp.sum(lam_q2[l] * lam_k2[l]))
               + lambda_init).astype(F32)

        dk, kvn, ikn, dqT, dvT, sqT, kvT, iqT, iwT = _inproj(x, mod3, g_norm_mix[l], w_in[l], g_kv_norm[l])
        odT = _diff_attention(dk, dqT, dvT, lam, bias_diff, g_subln[l], lambda_init)
        osT = _dsa_attention(ikn, iqT, iwT, kvn, kvT, sqT, w_uv[l], bias_dsa)
        x1, h2 = _outproj(x, odT, osT, w_out[l], mod3, g_norm_ffn[l])

        T = B * S
        h2f = h2.reshape(T, D)
        idx, g = _route(h2f, w_peer_q[l], peer_keys[l])
        w = _expert_act(idx, h2f.reshape(T, 8, 128), g, _pack_rows(peer_u[l]))
        peer_out = _expert_out(idx, w, _pack_rows(peer_v[l])).reshape(B, S, D)
        x = _final(x1, peer_out, mod3, g_final, last_layer=(l == depth - 1))
    return x
```
